```python
import numpy as np
import jax
import jax.numpy as jnp
from jax import lax

D_MODEL = 2048
BATCH = 4
SEQ = 8192
DEPTH = 2

GRID_W = 64
CTX_LEN = 256
D_MIX = D_MODEL
NA_HEAD_DIM = 128
NA_HEADS = (D_MIX // 2) // NA_HEAD_DIM
NA_WIDTH = NA_HEADS * NA_HEAD_DIM
NA_ROWS = 8
NA_COLS = 16
NA_KEY_COLS = 2 * NA_COLS
HG_WIDTH = D_MIX // 4
HG_EXPAND = 128
HG_HEADS = HG_WIDTH // HG_EXPAND
HG_HEAD_V = HG_WIDTH // HG_HEADS
GLA_WIDTH = D_MIX - NA_WIDTH - HG_WIDTH
GLA_HEADS = 4
GLA_KEY_WIDTH = GLA_WIDTH // 2
GLA_HEAD_K = GLA_KEY_WIDTH // GLA_HEADS
GLA_HEAD_V = GLA_WIDTH // GLA_HEADS
GLA_RANK = 16
GLA_GATE_NORM = 16.0
CHUNK = 64
D_FF = 5632
CONV_WIDTH = 3
ROPE_BASE = 10000.0
EPS = 1e-6
SPLIT_SIZES = (NA_WIDTH, NA_WIDTH, NA_WIDTH,
               HG_WIDTH, HG_WIDTH, HG_WIDTH, HG_WIDTH, HG_WIDTH,
               GLA_KEY_WIDTH, GLA_KEY_WIDTH, GLA_WIDTH, GLA_WIDTH, GLA_RANK, GLA_RANK)
IN_COLS = sum(SPLIT_SIZES)

kernel_name = "hybrid_na_hgrn2_gla_prefix_dit"


def rms_norm(x, g):
    xf = x.astype(jnp.float32)
    y = xf * lax.rsqrt(jnp.mean(xf * xf, axis=-1, keepdims=True) + EPS)
    return (y * g.astype(jnp.float32)).astype(x.dtype)


def modulate(h, shift, scale):
    return h * (1.0 + scale) + shift


def heads(a, n):
    return a.reshape(*a.shape[:-1], n, a.shape[-1] // n)


def split_points():
    return [int(v) for v in np.cumsum(SPLIT_SIZES)[:-1]]


def axial_rope(x, rows, cols):
    half = x.shape[-1] // 2
    inv = ROPE_BASE ** (-jnp.arange(0, half, 2, dtype=jnp.float32) / half)

    def rot(xp, pos):
        ang = pos.astype(jnp.float32)[:, None] * inv
        cos = jnp.cos(ang)[None, :, None, :]
        sin = jnp.sin(ang)[None, :, None, :]
        x1, x2 = jnp.split(xp.astype(jnp.float32), 2, axis=-1)
        return jnp.concatenate([x1 * cos - x2 * sin, x1 * sin + x2 * cos], axis=-1)

    return jnp.concatenate([rot(x[..., :half], rows), rot(x[..., half:], cols)], axis=-1).astype(x.dtype)


def hgrn_lower_bounds(p):
    s = jax.nn.softmax(p.astype(jnp.float32), axis=1)
    return jnp.cumsum(s, axis=1) - s[:, :1]


def hgrn2_forget(z, lb):
    zf = z.astype(jnp.float32)
    f = lb + (1.0 - lb) * jax.nn.sigmoid(zf)
    return (1.0 - lb) * jax.nn.sigmoid(-zf), jnp.log(f)


def gla_log_gate(r, up, b):
    return jax.nn.log_sigmoid((r @ up + b).astype(jnp.float32)) / GLA_GATE_NORM


def chunk_scan(q, k, v, log_g, s0):
    b_, t_, h_, _ = q.shape
    dv = v.shape[-1]
    n = t_ // CHUNK

    def to_chunks(a):
        return a.astype(jnp.float32).reshape(b_, n, CHUNK, h_, a.shape[-1]).transpose(1, 0, 3, 2, 4)

    lower = jnp.tril(jnp.ones((CHUNK, CHUNK), dtype=bool))[:, :, None]

    def step(s, blk):
        qc, kc, vc, gc = blk
        cum = jnp.cumsum(gc, axis=2)
        o_inter = jnp.einsum('bhtd,bhde->bhte', qc * jnp.exp(cum), s)
        rel = cum[:, :, :, None, :] - cum[:, :, None, :, :]
        decay = jnp.exp(jnp.where(lower, rel, -jnp.inf))
        att = jnp.einsum('bhtd,bhsd,bhtsd->bhts', qc, kc, decay)
        o = o_inter + jnp.einsum('bhts,bhse->bhte', att, vc)
        last = cum[:, :, -1:, :]
        s_new = jnp.exp(last[:, :, 0, :, None]) * s + jnp.einsum('bhsd,bhse->bhde', kc * jnp.exp(last - cum), vc)
        return s_new, o

    s_fin, o = lax.scan(step, s0, (to_chunks(q), to_chunks(k), to_chunks(v), to_chunks(log_g)))
    o = o.transpose(1, 0, 3, 2, 4).reshape(b_, t_, h_, dv)
    return o.astype(v.dtype), s_fin


def bidir_prefix_scan(q, v, k_f, g_f, k_b, g_b, cq, cv, ck_f, cg_f, ck_b, cg_b):
    b_, _, h_, dk = q.shape
    s0 = jnp.zeros((b_, h_, dk, v.shape[-1]), jnp.float32)
    rev = lambda a: a[:, ::-1]
    c_of, s_f = chunk_scan(cq, ck_f, cv, cg_f, s0)
    c_ob, s_b = chunk_scan(rev(cq), rev(ck_b), rev(cv), rev(cg_b), s0)
    o_f, _ = chunk_scan(q, k_f, v, g_f, s_f)
    o_b, _ = chunk_scan(rev(q), rev(k_b), rev(v), rev(g_b), s_b)
    return o_f + rev(o_b), c_of + rev(c_ob)


def neighborhood_attention(q, k, v, k_ctx, v_ctx, rpb):
    b_, t_, h_, d = q.shape
    rows = t_ // GRID_W
    kr = min(NA_ROWS, rows)
    n_cb = GRID_W // NA_COLS
    scale = d ** -0.5
    to_grid = lambda a: a.transpose(0, 2, 1, 3).reshape(b_, h_, rows, GRID_W, d)
    qg, kg, vg = to_grid(q), to_grid(k), to_grid(v)
    kc = k_ctx.transpose(0, 2, 1, 3)
    vc = v_ctx.transpose(0, 2, 1, 3)
    qcol = np.arange(GRID_W).reshape(n_cb, NA_COLS)
    cstart = np.clip(qcol - NA_COLS // 2, 0, GRID_W - NA_COLS)
    kstart = np.clip(np.arange(n_cb) * NA_COLS - NA_COLS // 2, 0, GRID_W - NA_KEY_COLS)
    kcol = kstart[:, None] + np.arange(NA_KEY_COLS)
    col_ok = (kcol[:, None, :] >= cstart[:, :, None]) & (kcol[:, None, :] < cstart[:, :, None] + NA_COLS)
    col_ok = jnp.asarray(col_ok[:, :, None, :])
    dc_idx = np.clip(kcol[:, None, :] - qcol[:, :, None], -(NA_COLS - 1), NA_COLS - 1) + NA_COLS - 1
    rpb_g = rpb[:, :, dc_idx].transpose(0, 2, 3, 1, 4)
    n_loc = kr * NA_KEY_COLS

    def row_fn(r):
        rs = jnp.clip(r - kr // 2, 0, rows - kr)
        q_r = lax.dynamic_index_in_dim(qg, r, axis=2, keepdims=False).reshape(b_, h_, n_cb, NA_COLS, d)
        k_blk = lax.dynamic_slice_in_dim(kg, rs, kr, axis=2)[:, :, :, kcol]
        v_blk = lax.dynamic_slice_in_dim(vg, rs, kr, axis=2)[:, :, :, kcol]
        bias = jnp.take(rpb_g, rs + jnp.arange(kr) - r + NA_ROWS - 1, axis=3)
        s_loc = jnp.einsum('bhjqd,bhrjkd->bhjqrk', q_r, k_blk).astype(jnp.float32) * scale + bias
        s_loc = jnp.where(col_ok, s_loc, -jnp.inf)
        s_ctx = jnp.einsum('bhjqd,bhld->bhjql', q_r, kc).astype(jnp.float32) * scale
        s = jnp.concatenate([s_loc.reshape(b_, h_, n_cb, NA_COLS, n_loc), s_ctx], axis=-1)
        p = jax.nn.softmax(s, axis=-1).astype(v.dtype)
        p_loc = p[..., :n_loc].reshape(b_, h_, n_cb, NA_COLS, kr, NA_KEY_COLS)
        o = (jnp.einsum('bhjqrk,bhrjkd->bhjqd', p_loc, v_blk)
             + jnp.einsum('bhjql,bhld->bhjqd', p[..., n_loc:], vc))
        return o.reshape(b_, h_, GRID_W, d)

    out = lax.map(row_fn, jnp.arange(rows))
    return out.transpose(1, 0, 3, 2, 4).reshape(b_, t_, h_, d)


def context_attention(q, k, v):
    s = jnp.einsum('blhd,bmhd->bhlm', q, k).astype(jnp.float32) * (q.shape[-1] ** -0.5)
    p = jax.nn.softmax(s, axis=-1).astype(v.dtype)
    return jnp.einsum('bhlm,bmhd->blhd', p, v)


def mixer_features(h, w_in, lb_f, lb_b, gate_up, gate_b):
    p = h @ w_in
    (na_q, na_k, na_v, hg_q, hg_i, hg_zf, hg_zb, hg_g,
     gl_q, gl_k, gl_v, gl_g, gl_rf, gl_rb) = jnp.split(p, split_points(), axis=-1)
    hg_kf, hg_lf = hgrn2_forget(hg_zf, lb_f)
    hg_kb, hg_lb = hgrn2_forget(hg_zb, lb_b)
    return dict(
        na_q=heads(na_q, NA_HEADS), na_k=heads(na_k, NA_HEADS), na_v=heads(na_v, NA_HEADS),
        hg_q=heads(jax.nn.silu(hg_q), HG_HEADS), hg_v=heads(hg_i, HG_HEADS),
        hg_kf=heads(hg_kf, HG_HEADS), hg_lf=heads(hg_lf, HG_HEADS),
        hg_kb=heads(hg_kb, HG_HEADS), hg_lb=heads(hg_lb, HG_HEADS), hg_g=hg_g,
        gl_q=heads(gl_q * (GLA_HEAD_K ** -0.5), GLA_HEADS), gl_k=heads(gl_k, GLA_HEADS),
        gl_v=heads(gl_v, GLA_HEADS), gl_g=gl_g,
        gl_lf=heads(gla_log_gate(gl_rf, gate_up[0], gate_b[0]), GLA_HEADS),
        gl_lb=heads(gla_log_gate(gl_rb, gate_up[1], gate_b[1]), GLA_HEADS),
    )


def norm_gate(o, norm_g, g):
    o = rms_norm(o, norm_g)
    return o.reshape(*o.shape[:2], -1) * jax.nn.silu(g)


def conv_ffn(h, w_up, conv_w, conv_b, w_down):
    u = h @ w_up
    u = lax.conv_general_dilated(u, conv_w[:, None, :], window_strides=(1,),
                                 padding=((CONV_WIDTH // 2, CONV_WIDTH // 2),),
                                 dimension_numbers=('NWC', 'WIO', 'NWC'),
                                 feature_group_count=u.shape[-1]) + conv_b
    a, b = jnp.split(u, 2, axis=-1)
    return (jax.nn.silu(a) * b) @ w_down


def trunk_layer(x, xc, mod, mod_c, norm1_g, w_in, rpb, lb_f, lb_b, gate_up, gate_b,
                hg_norm_g, gla_norm_g, w_out, norm2_g, w_up, conv_w, conv_b, w_down, update_ctx):
    sh1, sc1, gt1, sh2, sc2, gt2 = [m[:, None] for m in jnp.split(mod, 6, axis=-1)]
    csh1, csc1, cgt1, csh2, csc2, cgt2 = jnp.split(mod_c, 6, axis=-1)

    lat = mixer_features(modulate(rms_norm(x, norm1_g), sh1, sc1), w_in, lb_f, lb_b, gate_up, gate_b)
    cf = mixer_features(modulate(rms_norm(xc, norm1_g), csh1, csc1), w_in, lb_f, lb_b, gate_up, gate_b)

    t_ = x.shape[1]
    pos = jnp.arange(t_)
    rows, cols = pos // GRID_W, pos % GRID_W
    gl_q = axial_rope(lat['gl_q'], rows, cols)
    gl_k = axial_rope(lat['gl_k'], rows, cols)

    na_o = neighborhood_attention(lat['na_q'], lat['na_k'], lat['na_v'], cf['na_k'], cf['na_v'], rpb)
    hg_o, hg_co = bidir_prefix_scan(lat['hg_q'], lat['hg_v'], lat['hg_kf'], lat['hg_lf'], lat['hg_kb'], lat['hg_lb'],
                                    cf['hg_q'], cf['hg_v'], cf['hg_kf'], cf['hg_lf'], cf['hg_kb'], cf['hg_lb'])
    gl_o, gl_co = bidir_prefix_scan(gl_q, lat['gl_v'], gl_k, lat['gl_lf'], gl_k, lat['gl_lb'],
                                    cf['gl_q'], cf['gl_v'], cf['gl_k'], cf['gl_lf'], cf['gl_k'], cf['gl_lb'])

    mix = jnp.concatenate([na_o.reshape(*na_o.shape[:2], NA_WIDTH),
                           norm_gate(hg_o, hg_norm_g, lat['hg_g']),
                           norm_gate(gl_o, gla_norm_g, lat['gl_g'])], axis=-1) @ w_out
    x = x + gt1 * mix
    x = x + gt2 * conv_ffn(modulate(rms_norm(x, norm2_g), sh2, sc2), w_up, conv_w, conv_b, w_down)

    if update_ctx:
        c_na = context_attention(cf['na_q'], cf['na_k'], cf['na_v'])
        c_mix = jnp.concatenate([c_na.reshape(*c_na.shape[:2], NA_WIDTH),
                                 norm_gate(hg_co, hg_norm_g, cf['hg_g']),
                                 norm_gate(gl_co, gla_norm_g, cf['gl_g'])], axis=-1) @ w_out
        xc = xc + cgt1 * c_mix
        xc = xc + cgt2 * conv_ffn(modulate(rms_norm(xc, norm2_g), csh2, csc2), w_up, conv_w, conv_b, w_down)
    return x, xc


def setup_inputs(seed: int = 0) -> dict:
    key = jax.random.key(seed)
    ks = jax.random.split(key, 24)
    f32 = jnp.float32
    nrm = lambda k, shape, s: (jax.random.normal(k, shape, f32) * s)
    d = D_MODEL
    return {
        "x": nrm(ks[0], (BATCH, SEQ, d), 1.0),
        "c": nrm(ks[1], (BATCH, d), 1.0),
        "ctx": nrm(ks[2], (BATCH, CTX_LEN, d), 1.0),
        "c_ctx": nrm(ks[3], (d,), 1.0),
        "ada_w": nrm(ks[4], (DEPTH, d, 6 * d), d ** -0.5),
        "ada_b": nrm(ks[5], (DEPTH, 6 * d), 0.01),
        "norm1_g": 1.0 + nrm(ks[6], (DEPTH, d), 0.05),
        "w_in": nrm(ks[7], (DEPTH, d, IN_COLS), d ** -0.5),
        "na_rpb": nrm(ks[8], (DEPTH, NA_HEADS, 2 * NA_ROWS - 1, 2 * NA_COLS - 1), 0.1),
        "hg_lower_bounds": nrm(ks[9], (2, DEPTH, HG_WIDTH), 1.0),
        "hg_norm_g": 1.0 + nrm(ks[10], (DEPTH, HG_HEAD_V), 0.05),
        "gla_gate_up": nrm(ks[11], (DEPTH, 2, GLA_RANK, GLA_KEY_WIDTH), GLA_RANK ** -0.5),
        "gla_gate_b": nrm(ks[12], (DEPTH, 2, GLA_KEY_WIDTH), 0.1),
        "gla_norm_g": 1.0 + nrm(ks[13], (DEPTH, GLA_HEAD_V), 0.05),
        "w_out": nrm(ks[14], (DEPTH, D_MIX, d), D_MIX ** -0.5),
        "norm2_g": 1.0 + nrm(ks[15], (DEPTH, d), 0.05),
        "w_up": nrm(ks[16], (DEPTH, d, 2 * D_FF), d ** -0.5),
        "conv_w": nrm(ks[17], (DEPTH, CONV_WIDTH, 2 * D_FF), CONV_WIDTH ** -0.5),
        "conv_b": nrm(ks[18], (DEPTH, 2 * D_FF), 0.01),
        "w_down": nrm(ks[19], (DEPTH, D_FF, d), D_FF ** -0.5),
        "final_g": 1.0 + nrm(ks[20], (d,), 0.05),
    }


def reference(x, c, ctx, c_ctx, ada_w, ada_b, norm1_g, w_in, na_rpb, hg_lower_bounds, hg_norm_g,
              gla_gate_up, gla_gate_b, gla_norm_g, w_out, norm2_g, w_up, conv_w, conv_b, w_down, final_g):
    lb = hgrn_lower_bounds(hg_lower_bounds)
    sc = jax.nn.silu(c)
    scc = jax.nn.silu(c_ctx)
    xc = ctx
    for l in range(DEPTH):
        mod = sc @ ada_w[l] + ada_b[l]
        mod_c = scc @ ada_w[l] + ada_b[l]
        x, xc = trunk_layer(x, xc, mod, mod_c, norm1_g[l], w_in[l], na_rpb[l], lb[0, l], lb[1, l],
                            gla_gate_up[l], gla_gate_b[l], hg_norm_g[l], gla_norm_g[l], w_out[l],
                            norm2_g[l], w_up[l], conv_w[l], conv_b[l], w_down[l], l < DEPTH - 1)
    return rms_norm(x, final_g)
```

```python
import functools

import numpy as np
import jax
import jax.numpy as jnp
from jax import lax
from jax.experimental import pallas as pl
from jax.experimental.pallas import tpu as pltpu

F32 = jnp.float32
BF = jnp.bfloat16

D_MODEL = 2048
GRID_W = 64
NA_HEADS = 8
NA_HEAD_DIM = 128
NA_WIDTH = NA_HEADS * NA_HEAD_DIM
NA_ROWS = 8
NA_COLS = 16
HG_WIDTH = 512
HG_HEADS = 4
GLA_WIDTH = 512
GLA_HEADS = 4
GLA_KEY_WIDTH = 256
GLA_HEAD_K = 64
GLA_RANK = 16
GLA_GATE_NORM = 16.0
D_FF = 5632
ROPE_BASE = 10000.0
EPS = 1e-6
MAIN_COLS = 7168
COL_NA_Q, COL_NA_K, COL_NA_V = 0, 1024, 2048
COL_HG_Q, COL_HG_I, COL_HG_ZF, COL_HG_ZB, COL_HG_G = 3072, 3584, 4096, 4608, 5120
COL_GL_Q, COL_GL_K, COL_GL_V, COL_GL_G = 5632, 5888, 6144, 6656

LANES = 128
SUBLANES = 8
VMEM_LIMIT = 56 * 1024 * 1024

NA_QROWS = 8
NA_KROWS = 16
SCAN_CHUNK = 256
SCAN_BLOCK = 16
NEG = -1e30


def _dot(a, b):
    return jnp.dot(a, b, preferred_element_type=F32)


def _dot_nt(a, b):
    return lax.dot_general(a, b, (((1,), (1,)), ((), ())), preferred_element_type=F32)


def _dot_tn(a, b):
    return lax.dot_general(a, b, (((0,), (0,)), ((), ())), preferred_element_type=F32)


def _sigmoid(x):
    return 1.0 / (1.0 + jnp.exp(-x))


def _rms(x):
    return x * lax.rsqrt(jnp.mean(x * x, axis=-1, keepdims=True) + EPS)


def _params(sem):
    return pltpu.CompilerParams(dimension_semantics=sem, vmem_limit_bytes=VMEM_LIMIT)


def _ada_kernel(c_ref, w_ref, b_ref, o_ref):
    c = c_ref[...]
    s = (c * _sigmoid(c)).astype(BF)
    o_ref[0] = _dot(s, w_ref[0].astype(BF)) + b_ref[0]


def _ada(c_all, ada_w, ada_b):
    depth, d, n = ada_w.shape
    tn = 1024
    return pl.pallas_call(
        _ada_kernel,
        grid=(depth, n // tn),
        in_specs=[pl.BlockSpec((SUBLANES, d), lambda l, j: (0, 0)),
                  pl.BlockSpec((1, d, tn), lambda l, j: (l, 0, j)),
                  pl.BlockSpec((1, 1, tn), lambda l, j: (l, 0, j))],
        out_specs=pl.BlockSpec((1, SUBLANES, tn), lambda l, j: (l, 0, j)),
        out_shape=jax.ShapeDtypeStruct((depth, SUBLANES, n), F32),
        compiler_params=_params(("parallel", "parallel")),
        name="ada",
    )(c_all, ada_w, ada_b.reshape(depth, 1, n))


def _inproj_kernel(x_ref, g_ref, sh_ref, sc_ref, w_ref, wr_ref, p_ref, r_ref, h_ref):
    @pl.when(pl.program_id(1) == 0)
    def _():
        y = _rms(x_ref[...]) * g_ref[...]
        h = (y * (1.0 + sc_ref[0]) + sh_ref[0]).astype(BF)
        h_ref[...] = h
        r_ref[...] = _dot(h, wr_ref[...])

    p_ref[...] = _dot(h_ref[...], w_ref[...])


def _inproj(x_all, norm_g, mod, w_main, w_rank, *, tiles_per_batch, n_batch):
    n, d = x_all.shape
    tm, tn = 512, 1024
    mod_row = lambda k: (lambda i, j: (jnp.minimum(i // tiles_per_batch, n_batch) * 6 + k, 0, 0))
    return pl.pallas_call(
        _inproj_kernel,
        grid=(n // tm, MAIN_COLS // tn),
        in_specs=[pl.BlockSpec((tm, d), lambda i, j: (i, 0)),
                  pl.BlockSpec((1, d), lambda i, j: (0, 0)),
                  pl.BlockSpec((1, 1, d), mod_row(0)),
                  pl.BlockSpec((1, 1, d), mod_row(1)),
                  pl.BlockSpec((d, tn), lambda i, j: (0, j)),
                  pl.BlockSpec((d, LANES), lambda i, j: (0, 0))],
        out_specs=[pl.BlockSpec((tm, tn), lambda i, j: (i, j)),
                   pl.BlockSpec((tm, LANES), lambda i, j: (i, 0))],
        out_shape=[jax.ShapeDtypeStruct((n, MAIN_COLS), F32),
                   jax.ShapeDtypeStruct((n, LANES), F32)],
        scratch_shapes=[pltpu.VMEM((tm, d), BF)],
        compiler_params=_params(("parallel", "arbitrary")),
        name="inproj",
    )(x_all, norm_g, mod, mod, w_main, w_rank)


def _na_bias_table(rpb, rows):
    qr = np.arange(NA_QROWS)[:, None]
    kr = np.arange(NA_KROWS)[None, :]
    qc = np.arange(GRID_W)[:, None]
    kc = np.arange(GRID_W)[None, :]
    cstart = np.clip(qc - NA_COLS // 2, 0, GRID_W - NA_COLS)
    col_ok = (kc >= cstart) & (kc < cstart + NA_COLS)
    dc_idx = np.clip(kc - qc, -(NA_COLS - 1), NA_COLS - 1) + NA_COLS - 1
    tables = []
    for r0 in (0, NA_QROWS, rows - NA_QROWS):
        ks = int(np.clip(r0 - NA_ROWS // 2, 0, rows - NA_KROWS))
        r = r0 + qr
        s = ks + kr
        rs = np.clip(r - NA_ROWS // 2, 0, rows - NA_ROWS)
        row_ok = (s >= rs) & (s < rs + NA_ROWS)
        dr_idx = np.clip(s - r + NA_ROWS - 1, 0, 2 * NA_ROWS - 2)
        ok = row_ok[:, None, :, None] & col_ok[None, :, None, :]
        b = rpb[:, dr_idx[:, None, :, None], dc_idx[None, :, None, :]]
        b = jnp.where(jnp.asarray(ok)[None], b, NEG)
        tables.append(b.reshape(rpb.shape[0], NA_QROWS * GRID_W, NA_KROWS * GRID_W))
    return jnp.stack(tables, axis=1)


def _na_kernel(q_ref, k_ref, v_ref, kc_ref, vc_ref, bias_ref, o_ref, kb_ref, vb_ref, *, rows):
    rb = pl.program_id(2)

    @pl.when(rb == 0)
    def _():
        kb_ref[...] = k_ref[...].astype(BF)
        vb_ref[...] = v_ref[...].astype(BF)

    nk = NA_KROWS * GRID_W
    ks = jnp.clip(rb * NA_QROWS - NA_ROWS // 2, 0, rows - NA_KROWS) * GRID_W
    ks = pl.multiple_of(ks, GRID_W)
    kblk = kb_ref[pl.ds(ks, nk), :]
    vblk = vb_ref[pl.ds(ks, nk), :]
    q = (q_ref[...] * (NA_HEAD_DIM ** -0.5)).astype(BF)
    s_loc = _dot_nt(q, kblk) + bias_ref[0, 0]
    s_ctx = _dot_nt(q, kc_ref[...].astype(BF))
    m = jnp.maximum(jnp.max(s_loc, axis=-1, keepdims=True), jnp.max(s_ctx, axis=-1, keepdims=True))
    p_loc = jnp.exp(s_loc - m)
    p_ctx = jnp.exp(s_ctx - m)
    l = jnp.sum(p_loc, axis=-1, keepdims=True) + jnp.sum(p_ctx, axis=-1, keepdims=True)
    o = _dot(p_loc.astype(BF), vblk) + _dot(p_ctx.astype(BF), vc_ref[...].astype(BF))
    o_ref[...] = (o / l).astype(o_ref.dtype)


def _na(p, bias, *, n_batch, seq, ctx_len, n_all):
    rows = seq // GRID_W
    tq = NA_QROWS * GRID_W
    rbs = seq // tq
    hd = NA_HEAD_DIM
    ctx_blk0 = (n_batch * seq) // ctx_len
    case = lambda rb: jnp.where(rb == 0, 0, jnp.where(rb == rbs - 1, 2, 1))
    return pl.pallas_call(
        functools.partial(_na_kernel, rows=rows),
        grid=(n_batch, NA_HEADS, rbs),
        in_specs=[pl.BlockSpec((tq, hd), lambda b, h, r: (b * rbs + r, COL_NA_Q // hd + h)),
                  pl.BlockSpec((seq, hd), lambda b, h, r: (b, COL_NA_K // hd + h)),
                  pl.BlockSpec((seq, hd), lambda b, h, r: (b, COL_NA_V // hd + h)),
                  pl.BlockSpec((ctx_len, hd), lambda b, h, r: (ctx_blk0 + b, COL_NA_K // hd + h)),
                  pl.BlockSpec((ctx_len, hd), lambda b, h, r: (ctx_blk0 + b, COL_NA_V // hd + h)),
                  pl.BlockSpec((1, 1, tq, NA_KROWS * GRID_W), lambda b, h, r: (h, case(r), 0, 0))],
        out_specs=pl.BlockSpec((tq, hd), lambda b, h, r: (b * rbs + r, h)),
        out_shape=jax.ShapeDtypeStruct((n_all, NA_WIDTH), BF),
        scratch_shapes=[pltpu.VMEM((seq, hd), BF), pltpu.VMEM((seq, hd), BF)],
        compiler_params=_params(("parallel", "parallel", "arbitrary")),
        name="na",
    )(p, p, p, p, p, bias)


def _cattn_kernel(q_ref, k_ref, v_ref, prev_ref, o_ref):
    del prev_ref
    q = (q_ref[...] * (NA_HEAD_DIM ** -0.5)).astype(BF)
    s = _dot_nt(q, k_ref[...].astype(BF))
    p = jnp.exp(s - jnp.max(s, axis=-1, keepdims=True))
    l = jnp.sum(p, axis=-1, keepdims=True)
    o_ref[...] = (_dot(p.astype(BF), v_ref[...].astype(BF)) / l).astype(o_ref.dtype)


def _cattn(p, na_o, *, n_batch, seq, ctx_len):
    hd = NA_HEAD_DIM
    blk0 = (n_batch * seq) // ctx_len
    return pl.pallas_call(
        _cattn_kernel,
        grid=(n_batch, NA_HEADS),
        in_specs=[pl.BlockSpec((ctx_len, hd), lambda b, h: (blk0 + b, COL_NA_Q // hd + h)),
                  pl.BlockSpec((ctx_len, hd), lambda b, h: (blk0 + b, COL_NA_K // hd + h)),
                  pl.BlockSpec((ctx_len, hd), lambda b, h: (blk0 + b, COL_NA_V // hd + h)),
                  pl.BlockSpec(memory_space=pl.ANY)],
        out_specs=pl.BlockSpec((ctx_len, hd), lambda b, h: (blk0 + b, h)),
        out_shape=jax.ShapeDtypeStruct(na_o.shape, na_o.dtype),
        input_output_aliases={3: 0},
        compiler_params=_params(("parallel", "parallel")),
        name="cattn",
    )(p, p, p, na_o)


def _bcast_rows(ref, idx_of_group, group, n_groups):
    return jnp.concatenate(
        [jnp.broadcast_to(ref[pl.ds(idx_of_group(g), 1), :], (group, ref.shape[-1]))
         for g in range(n_groups)], axis=0)


def _scan_chunk(q, k, lg, vs, masks, dirn, st_ref, cum_ref, k_ref, v_ref):
    cs = q.shape[0]
    nh = len(vs)
    sgn = 1 - 2 * dirn
    row = lax.broadcasted_iota(jnp.int32, (cs, cs), 0)
    col = lax.broadcasted_iota(jnp.int32, (cs, cs), 1)
    tri = jnp.where((row - col) * sgn >= 0, 1.0, 0.0).astype(BF)
    hi = lg.astype(BF)
    r1 = lg - hi.astype(F32)
    mid = r1.astype(BF)
    lo = (r1 - mid.astype(F32)).astype(BF)
    cum = _dot(tri, hi) + _dot(tri, mid) + _dot(tri, lo)

    cum_ref[...] = cum
    k_ref[...] = k
    for j in range(nh):
        v_ref[j] = vs[j]

    rowi = lax.broadcasted_iota(jnp.int32, (cs, LANES), 0)
    d_idx = lax.broadcasted_iota(jnp.int32, (LANES, LANES * nh), 0)
    j_idx = lax.broadcasted_iota(jnp.int32, (LANES, LANES * nh), 1)
    ones_mat = jnp.where(d_idx // (LANES // nh) == j_idx // LANES, 1.0, 0.0).astype(BF)

    bs = SCAN_BLOCK
    nb = cs // bs
    tl = rowi & (bs - 1)
    o_acc = [jnp.zeros((cs, LANES), F32) for _ in range(nh)]
    for s in range(bs):
        c_s = _bcast_rows(cum_ref, lambda b: b * bs + s, bs, nb)
        k_s = _bcast_rows(k_ref, lambda b: b * bs + s, bs, nb)
        ok = (tl - s) * sgn >= 0
        e = jnp.exp(jnp.where(ok, cum - c_s, NEG))
        r = _dot((q * k_s * e).astype(BF), ones_mat)
        for j in range(nh):
            v_s = _bcast_rows(v_ref.at[j], lambda b: b * bs + s, bs, nb)
            o_acc[j] = o_acc[j] + r[:, j * LANES:(j + 1) * LANES] * v_s

    att = [jnp.zeros((cs, cs), F32) for _ in range(nh)]
    m = bs
    while m < cs:
        pair = 2 * m
        cb = _bcast_rows(cum_ref, lambda p: p * pair + m - 1 + dirn, pair, cs // pair)
        x = cum - cb
        later = ((rowi & (pair - 1)) >= m).astype(jnp.int32) != dirn
        e = jnp.exp(jnp.where(later, x, -x))
        qt = jnp.where(later, q * e, 0.0)
        kt = jnp.where(later, 0.0, k * e).astype(BF)
        same_pair = (row ^ col) < pair
        for j in range(nh):
            qj = qt if masks[j] is None else qt * masks[j]
            att[j] = att[j] + jnp.where(same_pair, _dot_nt(qj.astype(BF), kt), 0.0)
        m = pair

    last = cum_ref[pl.ds((cs - 1) * (1 - dirn), 1), :]
    qh = q * jnp.exp(cum)
    kh = k * jnp.exp(last - cum)
    dec = jnp.exp(last)
    outs = []
    for j in range(nh):
        qj = qh if masks[j] is None else qh * masks[j]
        kj = kh if masks[j] is None else kh * masks[j]
        st = st_ref[j]
        vj = vs[j].astype(BF)
        o = o_acc[j] + _dot(att[j].astype(BF), vj) + _dot_nt(qj.astype(BF), st.astype(BF))
        st_ref[j] = st * dec + _dot_tn(vj, kj.astype(BF))
        outs.append(o)
    return outs


def _hg_kernel(q_ref, v_ref, z_ref, lbp_ref, o_ref, st_ref, cum_ref, k_ref, vs_ref, *, layer):
    dirn = pl.program_id(2)

    @pl.when(pl.program_id(3) == 0)
    def _():
        st_ref[...] = jnp.zeros_like(st_ref)

    p = lbp_ref[0]
    e = jnp.exp(p - jnp.max(p, axis=0, keepdims=True))
    den = jnp.sum(e, axis=0, keepdims=True)
    lb = jnp.zeros_like(den)
    for i in range(1, layer + 1):
        lb = lb + e[i:i + 1]
    lb = lb / den
    z = z_ref[...]
    en = jnp.exp(-jnp.abs(z))
    big = 1.0 / (1.0 + en)
    small = en * big
    sig = jnp.where(z >= 0, big, small)
    nsig = jnp.where(z >= 0, small, big)
    k = (1.0 - lb) * nsig
    lg = jnp.log(lb + (1.0 - lb) * sig)
    qr = q_ref[...]
    q = qr * _sigmoid(qr)
    (o,) = _scan_chunk(q, k, lg, [v_ref[...]], [None], dirn, st_ref, cum_ref, k_ref, vs_ref)
    o_ref[0] = o


def _gla_kernel(q_ref, k_ref, v_ref, r_ref, up_ref, gb_ref, cos_ref, sin_ref, o_ref,
                st_ref, cum_ref, ks_ref, vs_ref):
    dirn = pl.program_id(2)

    @pl.when(pl.program_id(3) == 0)
    def _():
        st_ref[...] = jnp.zeros_like(st_ref)

    cs = q_ref.shape[0]
    lane = lax.broadcasted_iota(jnp.int32, (cs, LANES), 1)
    first = (lane & 16) == 0
    cos = cos_ref[...]
    sin = sin_ref[...]

    def rope(x):
        partner = jnp.where(first, pltpu.roll(x, LANES - 16, 1), pltpu.roll(x, 16, 1))
        return x * cos + partner * sin

    q = rope(q_ref[...] * (GLA_HEAD_K ** -0.5))
    k = rope(k_ref[...])
    pre = _dot(r_ref[...].astype(BF), up_ref[0].astype(BF)) + gb_ref[0]
    lg = (jnp.minimum(pre, 0.0) - jnp.log(1.0 + jnp.exp(-jnp.abs(pre)))) * (1.0 / GLA_GATE_NORM)
    lane1 = lax.broadcasted_iota(jnp.int32, (1, LANES), 1)
    masks = [jnp.where(lane1 < GLA_HEAD_K, 1.0, 0.0), jnp.where(lane1 >= GLA_HEAD_K, 1.0, 0.0)]
    v = v_ref[...]
    outs = _scan_chunk(q, k, lg, [v[:, :LANES], v[:, LANES:]], masks, dirn,
                       st_ref, cum_ref, ks_ref, vs_ref)
    o_ref[0] = jnp.concatenate(outs, axis=1)


def _scan_rowblk(b, d, c, nc, ctx_blk0):
    ce = jnp.where(d == 0, c - 1, nc - c)
    return jnp.where(c == 0, ctx_blk0 + b, b * nc + ce)


def _scan_scratch(nh):
    cs = SCAN_CHUNK
    return [pltpu.VMEM((nh, LANES, LANES), F32), pltpu.VMEM((cs, LANES), F32),
            pltpu.VMEM((cs, LANES), F32), pltpu.VMEM((nh, cs, LANES), F32)]


def _hg_scan(p, lbp, *, layer, n_batch, seq, n_all):
    cs = SCAN_CHUNK
    nc = seq // cs
    blk0 = (n_batch * seq) // cs
    rb = lambda b, h, d, c: _scan_rowblk(b, d, c, nc, blk0)
    depth = lbp.shape[1]
    return pl.pallas_call(
        functools.partial(_hg_kernel, layer=layer),
        grid=(n_batch, HG_HEADS, 2, nc + 1),
        in_specs=[pl.BlockSpec((cs, LANES), lambda b, h, d, c: (rb(b, h, d, c), COL_HG_Q // LANES + h)),
                  pl.BlockSpec((cs, LANES), lambda b, h, d, c: (rb(b, h, d, c), COL_HG_I // LANES + h)),
                  pl.BlockSpec((cs, LANES),
                               lambda b, h, d, c: (rb(b, h, d, c), COL_HG_ZF // LANES + HG_HEADS * d + h)),
                  pl.BlockSpec((1, depth, LANES), lambda b, h, d, c: (d, 0, h))],
        out_specs=pl.BlockSpec((1, cs, LANES), lambda b, h, d, c: (d, rb(b, h, d, c), h)),
        out_shape=jax.ShapeDtypeStruct((2, n_all, HG_WIDTH), F32),
        scratch_shapes=_scan_scratch(1),
        compiler_params=_params(("parallel", "parallel", "parallel", "arbitrary")),
        name="hg_scan",
    )(p, p, p, lbp)


def _gla_scan(p, r, up_pad, gate_b, cos_t, sin_t, *, n_batch, seq, n_all):
    cs = SCAN_CHUNK
    nc = seq // cs
    blk0 = (n_batch * seq) // cs
    rb = lambda b, h, d, c: _scan_rowblk(b, d, c, nc, blk0)
    tb = lambda b, h, d, c: jnp.where(c == 0, nc, jnp.where(d == 0, c - 1, nc - c))
    return pl.pallas_call(
        _gla_kernel,
        grid=(n_batch, GLA_HEADS // 2, 2, nc + 1),
        in_specs=[pl.BlockSpec((cs, LANES), lambda b, h, d, c: (rb(b, h, d, c), COL_GL_Q // LANES + h)),
                  pl.BlockSpec((cs, LANES), lambda b, h, d, c: (rb(b, h, d, c), COL_GL_K // LANES + h)),
                  pl.BlockSpec((cs, 2 * LANES),
                               lambda b, h, d, c: (rb(b, h, d, c), COL_GL_V // (2 * LANES) + h)),
                  pl.BlockSpec((cs, LANES), lambda b, h, d, c: (rb(b, h, d, c), 0)),
                  pl.BlockSpec((1, LANES, LANES), lambda b, h, d, c: (d, 0, h)),
                  pl.BlockSpec((1, 1, LANES), lambda b, h, d, c: (d, 0, h)),
                  pl.BlockSpec((cs, LANES), lambda b, h, d, c: (tb(b, h, d, c), 0)),
                  pl.BlockSpec((cs, LANES), lambda b, h, d, c: (tb(b, h, d, c), 0))],
        out_specs=pl.BlockSpec((1, cs, 2 * LANES), lambda b, h, d, c: (d, rb(b, h, d, c), h)),
        out_shape=jax.ShapeDtypeStruct((2, n_all, GLA_WIDTH), F32),
        scratch_shapes=_scan_scratch(2),
        compiler_params=_params(("parallel", "parallel", "parallel", "arbitrary")),
        name="gla_scan",
    )(p, p, p, r, up_pad, gate_b, cos_t, sin_t)


def _rope_tables(seq, ctx_len):
    half = GLA_HEAD_K // 2
    inv = ROPE_BASE ** (-jnp.arange(0, half, 2, dtype=F32) / half)
    pos = jnp.arange(seq)
    ang_r = (pos // GRID_W).astype(F32)[:, None] * inv
    ang_c = (pos % GRID_W).astype(F32)[:, None] * inv
    cos = jnp.concatenate([jnp.cos(ang_r)] * 2 + [jnp.cos(ang_c)] * 2, axis=1)
    sin = jnp.concatenate([-jnp.sin(ang_r), jnp.sin(ang_r), -jnp.sin(ang_c), jnp.sin(ang_c)], axis=1)
    cos = jnp.concatenate([jnp.tile(cos, (1, 2)), jnp.ones((ctx_len, LANES), F32)], axis=0)
    sin = jnp.concatenate([jnp.tile(sin, (1, 2)), jnp.zeros((ctx_len, LANES), F32)], axis=0)
    return cos, sin


def _outproj_kernel(na_ref, ohg_ref, ghg_ref, ogl_ref, ggl_ref, nhg_ref, ngl_ref, w_ref, x_ref,
                    gt_ref, sh_ref, sc_ref, n2_ref, x1_ref, h2_ref, mix_ref):
    mix_ref[:, :NA_WIDTH] = na_ref[...]

    def norm_gate(o_ref, g_ref, gain_ref, col0, n_heads):
        o = o_ref[0] + o_ref[1]
        g = g_ref[...]
        for h in range(n_heads):
            sl = slice(h * LANES, (h + 1) * LANES)
            gh = g[:, sl]
            y = _rms(o[:, sl]) * gain_ref[...]
            mix_ref[:, col0 + h * LANES:col0 + (h + 1) * LANES] = (y * (gh * _sigmoid(gh))).astype(BF)

    norm_gate(ohg_ref, ghg_ref, nhg_ref, NA_WIDTH, HG_HEADS)
    norm_gate(ogl_ref, ggl_ref, ngl_ref, NA_WIDTH + HG_WIDTH, GLA_HEADS)
    x1 = x_ref[...] + gt_ref[0] * _dot(mix_ref[...], w_ref[...])
    x1_ref[...] = x1
    y = _rms(x1) * n2_ref[...]
    h2_ref[...] = (y * (1.0 + sc_ref[0]) + sh_ref[0]).astype(BF)


def _outproj(na_o, o_hg, o_gl, p, hg_gain, gl_gain, w_out, x_all, mod, norm2_g, *,
             n_rows, tiles_per_batch_of, n_batch):
    d = D_MODEL
    tm = 256
    tpb = tiles_per_batch_of(tm)
    mod_row = lambda k: (lambda i: (jnp.minimum(i // tpb, n_batch) * 6 + k, 0, 0))
    return pl.pallas_call(
        _outproj_kernel,
        grid=(n_rows // tm,),
        in_specs=[pl.BlockSpec((tm, NA_WIDTH), lambda i: (i, 0)),
                  pl.BlockSpec((2, tm, HG_WIDTH), lambda i: (0, i, 0)),
                  pl.BlockSpec((tm, HG_WIDTH), lambda i: (i, COL_HG_G // HG_WIDTH)),
                  pl.BlockSpec((2, tm, GLA_WIDTH), lambda i: (0, i, 0)),
                  pl.BlockSpec((tm, GLA_WIDTH), lambda i: (i, COL_GL_G // GLA_WIDTH)),
                  pl.BlockSpec((1, LANES), lambda i: (0, 0)),
                  pl.BlockSpec((1, LANES), lambda i: (0, 0)),
                  pl.BlockSpec((d, d), lambda i: (0, 0)),
                  pl.BlockSpec((tm, d), lambda i: (i, 0)),
                  pl.BlockSpec((1, 1, d), mod_row(2)),
                  pl.BlockSpec((1, 1, d), mod_row(3)),
                  pl.BlockSpec((1, 1, d), mod_row(4)),
                  pl.BlockSpec((1, d), lambda i: (0, 0))],
        out_specs=[pl.BlockSpec((tm, d), lambda i: (i, 0)),
                   pl.BlockSpec((tm, d), lambda i: (i, 0))],
        out_shape=[jax.ShapeDtypeStruct((n_rows, d), F32),
                   jax.ShapeDtypeStruct((n_rows, d), BF)],
        scratch_shapes=[pltpu.VMEM((tm, d), BF)],
        compiler_params=_params(("parallel",)),
        name="outproj",
    )(na_o, o_hg, p, o_gl, p, hg_gain, gl_gain, w_out, x_all, mod, mod, mod, norm2_g)


def _patch_row(x, r, row_in_slab, value, keep=None):
    r0 = (r // SUBLANES) * SUBLANES
    slab = x[r0:r0 + SUBLANES]
    new = value if keep is None else slab * keep + value * (1.0 - keep)
    slab = jnp.where(row_in_slab == r - r0, new, slab)
    parts = [x[:r0]] * (r0 > 0) + [slab] + [x[r0 + SUBLANES:]] * (r0 + SUBLANES < x.shape[0])
    return jnp.concatenate(parts, axis=0)


def _ffn_kernel(h_ref, halo_ref, wa_ref, wb_ref, cwa_ref, cwb_ref, cba_ref, cbb_ref, wd_ref,
                x1_ref, gt_ref, fg_ref, o_ref, acc_ref, *, n_lat_tiles, ctx_len, final_norm):
    i = pl.program_id(0)
    j = pl.program_id(1)
    tm = h_ref.shape[0]
    tf = wa_ref.shape[1]

    @pl.when(j == 0)
    def _():
        acc_ref[...] = jnp.zeros_like(acc_ref)

    h = h_ref[...]
    halo = halo_ref[0]
    row8 = lax.broadcasted_iota(jnp.int32, (SUBLANES, tf), 0)
    keep_lat = jnp.where(i < n_lat_tiles, 1.0, 0.0)
    zero_row = jnp.zeros((1, tf), F32)

    def conv(w_ref, cw_ref, cb_ref):
        u = _dot(h, w_ref[...])
        uh = _dot(halo, w_ref[...])
        prev = _patch_row(pltpu.roll(u, 1, 0), 0, row8, uh[0:1])
        nxt = _patch_row(pltpu.roll(u, tm - 1, 0), tm - 1, row8, uh[1:2])
        for r in range(ctx_len, tm, ctx_len):
            prev = _patch_row(prev, r, row8, zero_row, keep_lat)
            nxt = _patch_row(nxt, r - 1, row8, zero_row, keep_lat)
        cw = cw_ref[...]
        return cw[0:1] * prev + cw[1:2] * u + cw[2:3] * nxt + cb_ref[...]

    a = conv(wa_ref, cwa_ref, cba_ref)
    b = conv(wb_ref, cwb_ref, cbb_ref)
    g = (a * _sigmoid(a) * b).astype(BF)
    acc_ref[...] += _dot(g, wd_ref[...])

    @pl.when(j == pl.num_programs(1) - 1)
    def _():
        x2 = x1_ref[...] + gt_ref[0] * acc_ref[...]
        if final_norm:
            x2 = _rms(x2) * fg_ref[...]
        o_ref[...] = x2


def _ffn(h2, halo, w_up, conv_w, conv_b, w_down, x1, mod, final_g, *, n_rows, n_lat_tiles,
         tiles_per_batch, n_batch, ctx_len, final_norm):
    d = D_MODEL
    tm, tf = 512, 512
    nf = D_FF // tf
    mod_row = lambda i, j: (jnp.minimum(i // tiles_per_batch, n_batch) * 6 + 5, 0, 0)
    return pl.pallas_call(
        functools.partial(_ffn_kernel, n_lat_tiles=n_lat_tiles, ctx_len=ctx_len,
                          final_norm=final_norm),
        grid=(n_rows // tm, nf),
        in_specs=[pl.BlockSpec((tm, d), lambda i, j: (i, 0)),
                  pl.BlockSpec((1, 2 * SUBLANES, d), lambda i, j: (i, 0, 0)),
                  pl.BlockSpec((d, tf), lambda i, j: (0, j)),
                  pl.BlockSpec((d, tf), lambda i, j: (0, nf + j)),
                  pl.BlockSpec((3, tf), lambda i, j: (0, j)),
                  pl.BlockSpec((3, tf), lambda i, j: (0, nf + j)),
                  pl.BlockSpec((1, tf), lambda i, j: (0, j)),
                  pl.BlockSpec((1, tf), lambda i, j: (0, nf + j)),
                  pl.BlockSpec((tf, d), lambda i, j: (j, 0)),
                  pl.BlockSpec((tm, d), lambda i, j: (i, 0)),
                  pl.BlockSpec((1, 1, d), mod_row),
                  pl.BlockSpec((1, d), lambda i, j: (0, 0))],
        out_specs=pl.BlockSpec((tm, d), lambda i, j: (i, 0)),
        out_shape=jax.ShapeDtypeStruct((n_rows, d), F32),
        scratch_shapes=[pltpu.VMEM((tm, d), F32)],
        compiler_params=_params(("parallel", "arbitrary")),
        name="ffn",
    )(h2, halo, w_up, w_up, conv_w, conv_w, conv_b, conv_b, w_down, x1, mod, final_g)


def _ffn_halo(h2, tm, n_tiles, n_lat, seq, ctx_len):
    d = h2.shape[1]
    starts = np.arange(n_tiles) * tm
    seq_of = np.where(starts < n_lat, seq, ctx_len)
    rel = np.where(starts < n_lat, starts, starts - n_lat)
    prev_ok = jnp.asarray((rel % seq_of != 0).astype(np.float32))[:, None]
    next_ok = jnp.asarray(((rel + tm) % seq_of != 0).astype(np.float32))[:, None]
    hh = h2[:n_tiles * tm]
    zero = jnp.zeros((1, d), h2.dtype)
    prev = jnp.concatenate([zero, hh[tm - 1::tm][:-1]], axis=0) * prev_ok.astype(h2.dtype)
    nxt = jnp.concatenate([hh[tm::tm], zero], axis=0) * next_ok.astype(h2.dtype)
    pad = jnp.zeros((n_tiles, 2 * SUBLANES - 2, d), h2.dtype)
    return jnp.concatenate([prev[:, None], nxt[:, None], pad], axis=1)


def kernel(x, c, ctx, c_ctx, ada_w, ada_b, norm1_g, w_in, na_rpb, hg_lower_bounds, hg_norm_g,
           gla_gate_up, gla_gate_b, gla_norm_g, w_out, norm2_g, w_up, conv_w, conv_b, w_down, final_g):
    n_batch, seq, d = x.shape
    ctx_len = ctx.shape[1]
    depth = ada_w.shape[0]
    n_lat = n_batch * seq
    n_all = n_lat + n_batch * ctx_len
    assert d == D_MODEL and ctx_len == SCAN_CHUNK and seq % (NA_QROWS * GRID_W) == 0
    assert seq // GRID_W >= NA_KROWS + NA_QROWS and n_batch < SUBLANES

    x_all = jnp.concatenate([x.reshape(n_lat, d), ctx.reshape(n_batch * ctx_len, d)], axis=0)
    c_all = jnp.concatenate([c, c_ctx[None], jnp.zeros((SUBLANES - n_batch - 1, d), F32)], axis=0)
    mods = _ada(c_all, ada_w, ada_b)
    cos_t, sin_t = _rope_tables(seq, ctx_len)

    ffn_tm = 512
    for l in range(depth):
        last = l == depth - 1
        mod = mods[l].reshape(SUBLANES * 6, 1, d)
        w_main = w_in[l, :, :MAIN_COLS].astype(BF)
        w_rank = jnp.pad(w_in[l, :, MAIN_COLS:], ((0, 0), (0, LANES - 2 * GLA_RANK))).astype(BF)
        p, r = _inproj(x_all, norm1_g[l][None], mod, w_main, w_rank,
                       tiles_per_batch=seq // 512, n_batch=n_batch)

        bias = _na_bias_table(na_rpb[l], seq // GRID_W)
        na_o = _na(p, bias, n_batch=n_batch, seq=seq, ctx_len=ctx_len, n_all=n_all)
        if not last:
            na_o = _cattn(p, na_o, n_batch=n_batch, seq=seq, ctx_len=ctx_len)

        o_hg = _hg_scan(p, hg_lower_bounds, layer=l, n_batch=n_batch, seq=seq, n_all=n_all)
        up = gla_gate_up[l]
        up_pad = jnp.zeros((2, LANES, GLA_KEY_WIDTH), F32)
        up_pad = up_pad.at[0, :GLA_RANK].set(up[0]).at[1, GLA_RANK:2 * GLA_RANK].set(up[1])
        o_gl = _gla_scan(p, r, up_pad, gla_gate_b[l][:, None, :], cos_t, sin_t,
                         n_batch=n_batch, seq=seq, n_all=n_all)

        n_rows = n_lat if last else n_all
        x1, h2 = _outproj(na_o, o_hg, o_gl, p, hg_norm_g[l][None], gla_norm_g[l][None],
                          w_out[l].astype(BF), x_all, mod, norm2_g[l][None], n_rows=n_rows,
                          tiles_per_batch_of=lambda tm: seq // tm, n_batch=n_batch)
        n_tiles = n_rows // ffn_tm
        halo = _ffn_halo(h2, ffn_tm, n_tiles, n_lat, seq, ctx_len)
        x_all = _ffn(h2, halo, w_up[l].astype(BF), conv_w[l], conv_b[l][None], w_down[l].astype(BF),
                     x1, mod, final_g[None], n_rows=n_rows, n_lat_tiles=n_lat // ffn_tm,
                     tiles_per_batch=seq // ffn_tm, n_batch=n_batch, ctx_len=ctx_len,
                     final_norm=last)
    return x_all[:n_lat].reshape(n_batch, seq, d)
```

```python
import functools

import numpy as np
import jax
import jax.numpy as jnp
from jax import lax
from jax.experimental import pallas as pl
from jax.experimental.pallas import tpu as pltpu

F32 = jnp.float32
BF = jnp.bfloat16

D_MODEL = 2048
GRID_W = 64
NA_HEADS = 8
NA_HEAD_DIM = 128
NA_WIDTH = NA_HEADS * NA_HEAD_DIM
NA_ROWS = 8
NA_COLS = 16
HG_WIDTH = 512
HG_HEADS = 4
GLA_WIDTH = 512
GLA_HEADS = 4
GLA_KEY_WIDTH = 256
GLA_HEAD_K = 64
GLA_RANK = 16
GLA_GATE_NORM = 16.0
D_FF = 5632
ROPE_BASE = 10000.0
EPS = 1e-6
MAIN_COLS = 7168
NA_COLS_ALL = 3 * NA_WIDTH
COL_NA_Q, COL_NA_K, COL_NA_V = 0, 1024, 2048
COL_HG_Q, COL_HG_I, COL_HG_ZF, COL_HG_G = 0, 512, 1024, 2048
COL_GL_Q, COL_GL_K, COL_GL_V, COL_GL_G = 2560, 2816, 3072, 3584

LANES = 128
SUBLANES = 8
VMEM_LIMIT = 56 * 1024 * 1024

NA_QROWS = 8
NA_KROWS = 16
SCAN_CHUNK = 256
SCAN_BLOCK = 16
NEG = -1e30
LOG2E = 1.4426950408889634


def _dot(a, b):
    return jnp.dot(a, b, preferred_element_type=F32)


def _dot_nt(a, b):
    return lax.dot_general(a, b, (((1,), (1,)), ((), ())), preferred_element_type=F32)


def _dot_tn(a, b):
    return lax.dot_general(a, b, (((0,), (0,)), ((), ())), preferred_element_type=F32)


def _sigmoid(x):
    return 1.0 / (1.0 + jnp.exp(-x))


def _rms(x):
    return x * lax.rsqrt(jnp.mean(x * x, axis=-1, keepdims=True) + EPS)


def _params(sem):
    return pltpu.CompilerParams(dimension_semantics=sem, vmem_limit_bytes=VMEM_LIMIT)


def _ada_kernel(c_ref, w_ref, b_ref, o_ref):
    c = c_ref[...]
    s = (c * _sigmoid(c)).astype(BF)
    o_ref[0] = _dot(s, w_ref[0].astype(BF)) + b_ref[0]


def _ada(c_all, ada_w, ada_b):
    depth, d, n = ada_w.shape
    tn = 1024
    return pl.pallas_call(
        _ada_kernel,
        grid=(depth, n // tn),
        in_specs=[pl.BlockSpec((SUBLANES, d), lambda l, j: (0, 0)),
                  pl.BlockSpec((1, d, tn), lambda l, j: (l, 0, j)),
                  pl.BlockSpec((1, 1, tn), lambda l, j: (l, 0, j))],
        out_specs=pl.BlockSpec((1, SUBLANES, tn), lambda l, j: (l, 0, j)),
        out_shape=jax.ShapeDtypeStruct((depth, SUBLANES, n), F32),
        compiler_params=_params(("parallel", "parallel")),
        name="ada",
    )(c_all, ada_w, ada_b.reshape(depth, 1, n))


def _inproj_kernel(x_ref, g_ref, sh_ref, sc_ref, w_ref, wr_ref, pna_ref, prest_ref, r_ref, h_ref,
                   *, n_na_tiles):
    j = pl.program_id(1)

    @pl.when(j == 0)
    def _():
        y = _rms(x_ref[...]) * g_ref[...]
        h = (y * (1.0 + sc_ref[0]) + sh_ref[0]).astype(BF)
        h_ref[...] = h
        r_ref[...] = _dot(h, wr_ref[...])

    acc = _dot(h_ref[...], w_ref[...])

    @pl.when(j < n_na_tiles)
    def _():
        pna_ref[...] = acc.astype(BF)

    @pl.when(j >= n_na_tiles)
    def _():
        prest_ref[...] = acc


def _inproj(x_all, norm_g, mod, w_main, w_rank, *, seq, n_batch):
    n, d = x_all.shape
    tm, tn = 1024, 1024
    n_na = NA_COLS_ALL // tn
    tpb = seq // tm
    mod_row = lambda k: (lambda i, j: (jnp.minimum(i // tpb, n_batch) * 6 + k, 0, 0))
    return pl.pallas_call(
        functools.partial(_inproj_kernel, n_na_tiles=n_na),
        grid=(n // tm, MAIN_COLS // tn),
        in_specs=[pl.BlockSpec((tm, d), lambda i, j: (i, 0)),
                  pl.BlockSpec((1, d), lambda i, j: (0, 0)),
                  pl.BlockSpec((1, 1, d), mod_row(0)),
                  pl.BlockSpec((1, 1, d), mod_row(1)),
                  pl.BlockSpec((d, tn), lambda i, j: (0, j)),
                  pl.BlockSpec((d, LANES), lambda i, j: (0, 0))],
        out_specs=[pl.BlockSpec((tm, tn), lambda i, j: (i, jnp.minimum(j, n_na - 1))),
                   pl.BlockSpec((tm, tn), lambda i, j: (i, jnp.maximum(j - n_na, 0))),
                   pl.BlockSpec((tm, LANES), lambda i, j: (i, 0))],
        out_shape=[jax.ShapeDtypeStruct((n, NA_COLS_ALL), BF),
                   jax.ShapeDtypeStruct((n, MAIN_COLS - NA_COLS_ALL), F32),
                   jax.ShapeDtypeStruct((n, LANES), F32)],
        scratch_shapes=[pltpu.VMEM((tm, d), BF)],
        compiler_params=_params(("parallel", "arbitrary")),
        name="inproj",
    )(x_all, norm_g, mod, mod, w_main, w_rank)


def _na_bias_table(rpb, rows):
    n_heads = rpb.shape[0]
    n_dr = 2 * NA_ROWS - 1
    qc = np.arange(GRID_W)[:, None]
    kc = np.arange(GRID_W)[None, :]
    cstart = np.clip(qc - NA_COLS // 2, 0, GRID_W - NA_COLS)
    col_ok = (kc >= cstart) & (kc < cstart + NA_COLS)
    dc_idx = np.clip(kc - qc, -(NA_COLS - 1), NA_COLS - 1) + NA_COLS - 1
    onehot = (dc_idx[None] == np.arange(2 * NA_COLS - 1)[:, None, None]).astype(np.float32)
    by_col = jnp.einsum('hrj,jqk->hqrk', rpb, jnp.asarray(onehot), precision=lax.Precision.HIGHEST)
    by_col = jnp.where(jnp.asarray(col_ok)[None, :, None, :], by_col, NEG)
    pad = NA_KROWS
    flat = jnp.pad(by_col, ((0, 0), (0, 0), (pad, pad), (0, 0))).reshape(n_heads, GRID_W, -1)
    kr = np.arange(NA_KROWS)
    cases = []
    for r0 in (0, NA_QROWS, rows - NA_QROWS):
        ks = int(np.clip(r0 - NA_ROWS // 2, 0, rows - NA_KROWS))
        per_row = []
        for qr in range(NA_QROWS):
            r = r0 + qr
            rs = int(np.clip(r - NA_ROWS // 2, 0, rows - NA_ROWS))
            row_ok = np.repeat((ks + kr >= rs) & (ks + kr < rs + NA_ROWS), GRID_W)
            lo = ks - r + NA_ROWS - 1 + pad
            assert 0 <= lo and lo + NA_KROWS <= n_dr + 2 * pad
            sl = flat[:, :, lo * GRID_W:(lo + NA_KROWS) * GRID_W]
            per_row.append(jnp.where(jnp.asarray(row_ok)[None, None, :], sl, NEG))
        cases.append(jnp.stack(per_row, axis=1).reshape(n_heads, NA_QROWS * GRID_W, NA_KROWS * GRID_W))
    return jnp.stack(cases, axis=1)


def _na_kernel(q_ref, k_ref, v_ref, kc_ref, vc_ref, bias_ref, o_ref, *, rows):
    rb = pl.program_id(2)
    nk = NA_KROWS * GRID_W
    ks = jnp.clip(rb * NA_QROWS - NA_ROWS // 2, 0, rows - NA_KROWS) * GRID_W
    ks = pl.multiple_of(ks, GRID_W)
    kblk = k_ref[pl.ds(ks, nk), :]
    vblk = v_ref[pl.ds(ks, nk), :]
    q = (q_ref[...].astype(F32) * (NA_HEAD_DIM ** -0.5)).astype(BF)
    s_loc = _dot_nt(q, kblk) + bias_ref[0, 0]
    s_ctx = _dot_nt(q, kc_ref[...])
    m = jnp.maximum(jnp.max(s_loc, axis=-1, keepdims=True), jnp.max(s_ctx, axis=-1, keepdims=True))
    p_loc = jnp.exp(s_loc - m)
    p_ctx = jnp.exp(s_ctx - m)
    l = jnp.sum(p_loc, axis=-1, keepdims=True) + jnp.sum(p_ctx, axis=-1, keepdims=True)
    o = _dot(p_loc.astype(BF), vblk) + _dot(p_ctx.astype(BF), vc_ref[...])
    o_ref[...] = (o / l).astype(o_ref.dtype)


def _na(p, bias, *, n_batch, seq, ctx_len, n_all):
    rows = seq // GRID_W
    tq = NA_QROWS * GRID_W
    rbs = seq // tq
    hd = NA_HEAD_DIM
    ctx_blk0 = (n_batch * seq) // ctx_len
    case = lambda rb: jnp.where(rb == 0, 0, jnp.where(rb == rbs - 1, 2, 1))
    return pl.pallas_call(
        functools.partial(_na_kernel, rows=rows),
        grid=(n_batch, NA_HEADS, rbs),
        in_specs=[pl.BlockSpec((tq, hd), lambda b, h, r: (b * rbs + r, COL_NA_Q // hd + h)),
                  pl.BlockSpec((seq, hd), lambda b, h, r: (b, COL_NA_K // hd + h)),
                  pl.BlockSpec((seq, hd), lambda b, h, r: (b, COL_NA_V // hd + h)),
                  pl.BlockSpec((ctx_len, hd), lambda b, h, r: (ctx_blk0 + b, COL_NA_K // hd + h)),
                  pl.BlockSpec((ctx_len, hd), lambda b, h, r: (ctx_blk0 + b, COL_NA_V // hd + h)),
                  pl.BlockSpec((1, 1, tq, NA_KROWS * GRID_W), lambda b, h, r: (h, case(r), 0, 0))],
        out_specs=pl.BlockSpec((tq, hd), lambda b, h, r: (b * rbs + r, h)),
        out_shape=jax.ShapeDtypeStruct((n_all, NA_WIDTH), BF),
        compiler_params=_params(("parallel", "parallel", "arbitrary")),
        name="na",
    )(p, p, p, p, p, bias)


def _cattn_kernel(q_ref, k_ref, v_ref, prev_ref, o_ref):
    del prev_ref
    q = (q_ref[...].astype(F32) * (NA_HEAD_DIM ** -0.5)).astype(BF)
    s = _dot_nt(q, k_ref[...])
    p = jnp.exp(s - jnp.max(s, axis=-1, keepdims=True))
    l = jnp.sum(p, axis=-1, keepdims=True)
    o_ref[...] = (_dot(p.astype(BF), v_ref[...]) / l).astype(o_ref.dtype)


def _cattn(p, na_o, *, n_batch, seq, ctx_len):
    hd = NA_HEAD_DIM
    blk0 = (n_batch * seq) // ctx_len
    return pl.pallas_call(
        _cattn_kernel,
        grid=(n_batch, NA_HEADS),
        in_specs=[pl.BlockSpec((ctx_len, hd), lambda b, h: (blk0 + b, COL_NA_Q // hd + h)),
                  pl.BlockSpec((ctx_len, hd), lambda b, h: (blk0 + b, COL_NA_K // hd + h)),
                  pl.BlockSpec((ctx_len, hd), lambda b, h: (blk0 + b, COL_NA_V // hd + h)),
                  pl.BlockSpec(memory_space=pl.ANY)],
        out_specs=pl.BlockSpec((ctx_len, hd), lambda b, h: (blk0 + b, h)),
        out_shape=jax.ShapeDtypeStruct(na_o.shape, na_o.dtype),
        input_output_aliases={3: 0},
        compiler_params=_params(("parallel", "parallel")),
        name="cattn",
    )(p, p, p, na_o)


def _bcast_rows(ref, idx_of_group, group, n_groups):
    return jnp.concatenate(
        [jnp.broadcast_to(ref[idx_of_group(g):idx_of_group(g) + 1, :], (group, ref.shape[-1]))
         for g in range(n_groups)], axis=0)


def _scan_chunk(q, k, lg, vs, masks, dirn, st_ref, cum_ref, k_ref, v_ref):
    cs = q.shape[0]
    nh = len(vs)
    fwd = dirn == 0
    row = lax.broadcasted_iota(jnp.int32, (cs, cs), 0)
    col = lax.broadcasted_iota(jnp.int32, (cs, cs), 1)
    tri = jnp.where(row >= col if fwd else row <= col, 1.0, 0.0).astype(BF)
    hi = lg.astype(BF)
    r1 = lg - hi.astype(F32)
    mid = r1.astype(BF)
    lo = (r1 - mid.astype(F32)).astype(BF)
    cum = (_dot(tri, hi) + _dot(tri, mid) + _dot(tri, lo)) * LOG2E

    cum_ref[...] = cum
    k_ref[...] = k
    for j in range(nh):
        v_ref[j] = vs[j]

    d_idx = lax.broadcasted_iota(jnp.int32, (LANES, LANES * nh), 0)
    j_idx = lax.broadcasted_iota(jnp.int32, (LANES, LANES * nh), 1)
    ones_mat = jnp.where(d_idx // (LANES // nh) == j_idx // LANES, 1.0, 0.0).astype(BF)

    bs = SCAN_BLOCK
    nb = cs // bs
    n_half = bs // SUBLANES
    assert n_half == 2

    def half(x, h):
        return jnp.concatenate([x[b * bs + h * SUBLANES:b * bs + (h + 1) * SUBLANES] for b in range(nb)],
                               axis=0)

    q_h = [half(q, h) for h in range(n_half)]
    c_h = [half(cum, h) for h in range(n_half)]
    tl = lax.broadcasted_iota(jnp.int32, (nb * SUBLANES, LANES), 0) & (SUBLANES - 1)
    acc = [[jnp.zeros((nb * SUBLANES, LANES), F32) for _ in range(n_half)] for _ in range(nh)]
    for s in range(bs):
        s_half, s_loc = divmod(s, SUBLANES)
        at = lambda b: b * bs + s
        c_s = _bcast_rows(cum_ref, at, SUBLANES, nb)
        k_s = _bcast_rows(k_ref, at, SUBLANES, nb)
        v_s = [_bcast_rows(v_ref.at[j], at, SUBLANES, nb) for j in range(nh)]
        halves = [h for h in range(n_half) if (h >= s_half if fwd else h <= s_half)]
        parts = []
        for h in halves:
            x = c_h[h] - c_s
            if h == s_half and s_loc != (0 if fwd else SUBLANES - 1):
                x = jnp.where(tl >= s_loc if fwd else tl <= s_loc, x, NEG)
            parts.append((q_h[h] * k_s * jnp.exp2(x)).astype(BF))
        r = _dot(jnp.concatenate(parts, axis=0), ones_mat)
        for i, h in enumerate(halves):
            rh = r[i * nb * SUBLANES:(i + 1) * nb * SUBLANES]
            for j in range(nh):
                acc[j][h] = acc[j][h] + rh[:, j * LANES:(j + 1) * LANES] * v_s[j]
    o_acc = [jnp.concatenate([acc[j][h][b * SUBLANES:(b + 1) * SUBLANES]
                              for b in range(nb) for h in range(n_half)], axis=0) for j in range(nh)]

    rowi = lax.broadcasted_iota(jnp.int32, (cs, LANES), 0)
    att = [jnp.zeros((cs, cs), F32) for _ in range(nh)]
    m = bs
    while m < cs:
        pair = 2 * m
        bidx = m - 1 if fwd else m
        cb = _bcast_rows(cum_ref, lambda p: p * pair + bidx, pair, cs // pair)
        x = cum - cb
        local = rowi & (pair - 1)
        later = local >= m if fwd else local < m
        e = jnp.exp2(jnp.where(later, x, -x))
        qt = jnp.where(later, q * e, 0.0)
        kt = jnp.where(later, 0.0, k * e).astype(BF)
        for j in range(nh):
            qj = qt if masks[j] is None else qt * masks[j]
            a = _dot_nt(qj.astype(BF), kt)
            att[j] = att[j] + (a if pair == cs else jnp.where((row ^ col) < pair, a, 0.0))
        m = pair

    last_row = cs - 1 if fwd else 0
    last = cum_ref[last_row:last_row + 1, :]
    qh = q * jnp.exp2(cum)
    kh = k * jnp.exp2(last - cum)
    dec = jnp.exp2(last)
    outs = []
    for j in range(nh):
        qj = qh if masks[j] is None else qh * masks[j]
        kj = kh if masks[j] is None else kh * masks[j]
        st = st_ref[j]
        vj = vs[j].astype(BF)
        o = o_acc[j] + _dot(att[j].astype(BF), vj) + _dot_nt(qj.astype(BF), st.astype(BF))
        st_ref[j] = st * dec + _dot_tn(vj, kj.astype(BF))
        outs.append(o)
    return outs


def _hg_kernel(q_ref, v_ref, z_ref, lbp_ref, o_ref, st_ref, cum_ref, k_ref, vs_ref, *, layer):
    @pl.when(pl.program_id(3) == 0)
    def _():
        st_ref[...] = jnp.zeros_like(st_ref)

    p = lbp_ref[0]
    e = jnp.exp(p - jnp.max(p, axis=0, keepdims=True))
    den = jnp.sum(e, axis=0, keepdims=True)
    lb = jnp.zeros_like(den)
    for i in range(1, layer + 1):
        lb = lb + e[i:i + 1]
    lb = lb / den
    z = z_ref[...]
    en = jnp.exp(-jnp.abs(z))
    big = 1.0 / (1.0 + en)
    small = en * big
    sig = jnp.where(z >= 0, big, small)
    nsig = jnp.where(z >= 0, small, big)
    k = (1.0 - lb) * nsig
    lg = jnp.log(lb + (1.0 - lb) * sig)
    qr = q_ref[...]
    q = qr * _sigmoid(qr)
    for dirn in range(2):
        @pl.when(pl.program_id(2) == dirn)
        def _(dirn=dirn):
            (o,) = _scan_chunk(q, k, lg, [v_ref[...]], [None], dirn, st_ref, cum_ref, k_ref, vs_ref)
            o_ref[0] = o


def _gla_kernel(q_ref, k_ref, v_ref, r_ref, up_ref, gb_ref, cos_ref, sin_ref, o_ref,
                st_ref, cum_ref, ks_ref, vs_ref):
    @pl.when(pl.program_id(3) == 0)
    def _():
        st_ref[...] = jnp.zeros_like(st_ref)

    cs = q_ref.shape[0]
    lane = lax.broadcasted_iota(jnp.int32, (cs, LANES), 1)
    first = (lane & 16) == 0
    cos = cos_ref[...]
    sin = sin_ref[...]

    def rope(x):
        partner = jnp.where(first, pltpu.roll(x, LANES - 16, 1), pltpu.roll(x, 16, 1))
        return x * cos + partner * sin

    q = rope(q_ref[...] * (GLA_HEAD_K ** -0.5))
    k = rope(k_ref[...])
    pre = _dot(r_ref[...].astype(BF), up_ref[0].astype(BF)) + gb_ref[0]
    lg = (jnp.minimum(pre, 0.0) - jnp.log(1.0 + jnp.exp(-jnp.abs(pre)))) * (1.0 / GLA_GATE_NORM)
    lane1 = lax.broadcasted_iota(jnp.int32, (1, LANES), 1)
    masks = [jnp.where(lane1 < GLA_HEAD_K, 1.0, 0.0), jnp.where(lane1 >= GLA_HEAD_K, 1.0, 0.0)]
    v = v_ref[...]
    for dirn in range(2):
        @pl.when(pl.program_id(2) == dirn)
        def _(dirn=dirn):
            outs = _scan_chunk(q, k, lg, [v[:, :LANES], v[:, LANES:]], masks, dirn,
                               st_ref, cum_ref, ks_ref, vs_ref)
            o_ref[0] = jnp.concatenate(outs, axis=1)


def _scan_rowblk(b, d, c, nc, ctx_blk0):
    ce = jnp.where(d == 0, c - 1, nc - c)
    return jnp.where(c == 0, ctx_blk0 + b, b * nc + ce)


def _scan_scratch(nh):
    cs = SCAN_CHUNK
    return [pltpu.VMEM((nh, LANES, LANES), F32), pltpu.VMEM((cs, LANES), F32),
            pltpu.VMEM((cs, LANES), F32), pltpu.VMEM((nh, cs, LANES), F32)]


def _hg_scan(p, lbp, *, layer, n_batch, seq, n_all):
    cs = SCAN_CHUNK
    nc = seq // cs
    blk0 = (n_batch * seq) // cs
    rb = lambda b, h, d, c: _scan_rowblk(b, d, c, nc, blk0)
    depth = lbp.shape[1]
    return pl.pallas_call(
        functools.partial(_hg_kernel, layer=layer),
        grid=(n_batch, HG_HEADS, 2, nc + 1),
        in_specs=[pl.BlockSpec((cs, LANES), lambda b, h, d, c: (rb(b, h, d, c), COL_HG_Q // LANES + h)),
                  pl.BlockSpec((cs, LANES), lambda b, h, d, c: (rb(b, h, d, c), COL_HG_I // LANES + h)),
                  pl.BlockSpec((cs, LANES),
                               lambda b, h, d, c: (rb(b, h, d, c), COL_HG_ZF // LANES + HG_HEADS * d + h)),
                  pl.BlockSpec((1, depth, LANES), lambda b, h, d, c: (d, 0, h))],
        out_specs=pl.BlockSpec((1, cs, LANES), lambda b, h, d, c: (d, rb(b, h, d, c), h)),
        out_shape=jax.ShapeDtypeStruct((2, n_all, HG_WIDTH), F32),
        scratch_shapes=_scan_scratch(1),
        compiler_params=_params(("parallel", "parallel", "parallel", "arbitrary")),
        name="hg_scan",
    )(p, p, p, lbp)


def _gla_scan(p, r, up_pad, gate_b, cos_t, sin_t, *, n_batch, seq, n_all):
    cs = SCAN_CHUNK
    nc = seq // cs
    blk0 = (n_batch * seq) // cs
    rb = lambda b, h, d, c: _scan_rowblk(b, d, c, nc, blk0)
    tb = lambda b, h, d, c: jnp.where(c == 0, nc, jnp.where(d == 0, c - 1, nc - c))
    return pl.pallas_call(
        _gla_kernel,
        grid=(n_batch, GLA_HEADS // 2, 2, nc + 1),
        in_specs=[pl.BlockSpec((cs, LANES), lambda b, h, d, c: (rb(b, h, d, c), COL_GL_Q // LANES + h)),
                  pl.BlockSpec((cs, LANES), lambda b, h, d, c: (rb(b, h, d, c), COL_GL_K // LANES + h)),
                  pl.BlockSpec((cs, 2 * LANES),
                               lambda b, h, d, c: (rb(b, h, d, c), COL_GL_V // (2 * LANES) + h)),
                  pl.BlockSpec((cs, LANES), lambda b, h, d, c: (rb(b, h, d, c), 0)),
                  pl.BlockSpec((1, LANES, LANES), lambda b, h, d, c: (d, 0, h)),
                  pl.BlockSpec((1, 1, LANES), lambda b, h, d, c: (d, 0, h)),
                  pl.BlockSpec((cs, LANES), lambda b, h, d, c: (tb(b, h, d, c), 0)),
                  pl.BlockSpec((cs, LANES), lambda b, h, d, c: (tb(b, h, d, c), 0))],
        out_specs=pl.BlockSpec((1, cs, 2 * LANES), lambda b, h, d, c: (d, rb(b, h, d, c), h)),
        out_shape=jax.ShapeDtypeStruct((2, n_all, GLA_WIDTH), F32),
        scratch_shapes=_scan_scratch(2),
        compiler_params=_params(("parallel", "parallel", "parallel", "arbitrary")),
        name="gla_scan",
    )(p, p, p, r, up_pad, gate_b, cos_t, sin_t)


def _rope_tables(seq, ctx_len):
    half = GLA_HEAD_K // 2
    inv = ROPE_BASE ** (-jnp.arange(0, half, 2, dtype=F32) / half)
    pos = jnp.arange(seq)
    ang_r = (pos // GRID_W).astype(F32)[:, None] * inv
    ang_c = (pos % GRID_W).astype(F32)[:, None] * inv
    cos = jnp.concatenate([jnp.cos(ang_r)] * 2 + [jnp.cos(ang_c)] * 2, axis=1)
    sin = jnp.concatenate([-jnp.sin(ang_r), jnp.sin(ang_r), -jnp.sin(ang_c), jnp.sin(ang_c)], axis=1)
    cos = jnp.concatenate([jnp.tile(cos, (1, 2)), jnp.ones((ctx_len, LANES), F32)], axis=0)
    sin = jnp.concatenate([jnp.tile(sin, (1, 2)), jnp.zeros((ctx_len, LANES), F32)], axis=0)
    return cos, sin


def _outproj_kernel(na_ref, ohg_ref, ghg_ref, ogl_ref, ggl_ref, nhg_ref, ngl_ref, w_ref, x_ref,
                    gt_ref, sh_ref, sc_ref, n2_ref, x1_ref, h2_ref, mix_ref):
    mix_ref[:, :NA_WIDTH] = na_ref[...]

    def norm_gate(o_ref, g_ref, gain_ref, col0, n_heads):
        o = o_ref[0] + o_ref[1]
        g = g_ref[...]
        for h in range(n_heads):
            sl = slice(h * LANES, (h + 1) * LANES)
            gh = g[:, sl]
            y = _rms(o[:, sl]) * gain_ref[...]
            mix_ref[:, col0 + h * LANES:col0 + (h + 1) * LANES] = (y * (gh * _sigmoid(gh))).astype(BF)

    norm_gate(ohg_ref, ghg_ref, nhg_ref, NA_WIDTH, HG_HEADS)
    norm_gate(ogl_ref, ggl_ref, ngl_ref, NA_WIDTH + HG_WIDTH, GLA_HEADS)
    x1 = x_ref[...] + gt_ref[0] * _dot(mix_ref[...], w_ref[...])
    x1_ref[...] = x1
    y = _rms(x1) * n2_ref[...]
    h2_ref[...] = (y * (1.0 + sc_ref[0]) + sh_ref[0]).astype(BF)


def _outproj(na_o, o_hg, o_gl, p, hg_gain, gl_gain, w_out, x_all, mod, norm2_g, *,
             n_rows, seq, n_batch):
    d = D_MODEL
    tm = 256
    tpb = seq // tm
    mod_row = lambda k: (lambda i: (jnp.minimum(i // tpb, n_batch) * 6 + k, 0, 0))
    return pl.pallas_call(
        _outproj_kernel,
        grid=(n_rows // tm,),
        in_specs=[pl.BlockSpec((tm, NA_WIDTH), lambda i: (i, 0)),
                  pl.BlockSpec((2, tm, HG_WIDTH), lambda i: (0, i, 0)),
                  pl.BlockSpec((tm, HG_WIDTH), lambda i: (i, COL_HG_G // HG_WIDTH)),
                  pl.BlockSpec((2, tm, GLA_WIDTH), lambda i: (0, i, 0)),
                  pl.BlockSpec((tm, GLA_WIDTH), lambda i: (i, COL_GL_G // GLA_WIDTH)),
                  pl.BlockSpec((1, LANES), lambda i: (0, 0)),
                  pl.BlockSpec((1, LANES), lambda i: (0, 0)),
                  pl.BlockSpec((d, d), lambda i: (0, 0)),
                  pl.BlockSpec((tm, d), lambda i: (i, 0)),
                  pl.BlockSpec((1, 1, d), mod_row(2)),
                  pl.BlockSpec((1, 1, d), mod_row(3)),
                  pl.BlockSpec((1, 1, d), mod_row(4)),
                  pl.BlockSpec((1, d), lambda i: (0, 0))],
        out_specs=[pl.BlockSpec((tm, d), lambda i: (i, 0)),
                   pl.BlockSpec((tm, d), lambda i: (i, 0))],
        out_shape=[jax.ShapeDtypeStruct((n_rows, d), F32),
                   jax.ShapeDtypeStruct((n_rows, d), BF)],
        scratch_shapes=[pltpu.VMEM((tm, d), BF)],
        compiler_params=_params(("parallel",)),
        name="outproj",
    )(na_o, o_hg, p, o_gl, p, hg_gain, gl_gain, w_out, x_all, mod, mod, mod, norm2_g)


def _patch_row(x, r, row_in_slab, value, keep=None):
    r0 = (r // SUBLANES) * SUBLANES
    slab = x[r0:r0 + SUBLANES]
    new = value if keep is None else slab * keep + value * (1.0 - keep)
    slab = jnp.where(row_in_slab == r - r0, new, slab)
    parts = [x[:r0]] * (r0 > 0) + [slab] + [x[r0 + SUBLANES:]] * (r0 + SUBLANES < x.shape[0])
    return jnp.concatenate(parts, axis=0)


def _ffn_kernel(h_ref, halo_ref, wa_ref, wb_ref, cwa_ref, cwb_ref, cba_ref, cbb_ref, wd_ref,
                x1_ref, gt_ref, fg_ref, o_ref, acc_ref, hc_ref, *, n_lat_tiles, ctx_len, final_norm):
    i = pl.program_id(0)
    j = pl.program_id(1)
    tm = h_ref.shape[0]
    tf = wa_ref.shape[1]

    @pl.when(j == 0)
    def _():
        acc_ref[...] = jnp.zeros_like(acc_ref)
        hc_ref[:tm] = h_ref[...]
        hc_ref[tm:] = halo_ref[0]

    hc = hc_ref[...]
    row8 = lax.broadcasted_iota(jnp.int32, (SUBLANES, tf), 0)
    keep_lat = jnp.where(i < n_lat_tiles, 1.0, 0.0)
    zero_row = jnp.zeros((1, tf), F32)

    def conv(w_ref, cw_ref, cb_ref):
        u_all = _dot(hc, w_ref[...])
        u = u_all[:tm]
        prev = _patch_row(pltpu.roll(u, 1, 0), 0, row8, u_all[tm:tm + 1])
        nxt = _patch_row(pltpu.roll(u, tm - 1, 0), tm - 1, row8, u_all[tm + 1:tm + 2])
        for r in range(ctx_len, tm, ctx_len):
            prev = _patch_row(prev, r, row8, zero_row, keep_lat)
            nxt = _patch_row(nxt, r - 1, row8, zero_row, keep_lat)
        cw = cw_ref[...]
        return cw[0:1] * prev + cw[1:2] * u + cw[2:3] * nxt + cb_ref[...]

    a = conv(wa_ref, cwa_ref, cba_ref)
    b = conv(wb_ref, cwb_ref, cbb_ref)
    g = (a * _sigmoid(a) * b).astype(BF)
    acc_ref[...] += _dot(g, wd_ref[...])

    @pl.when(j == pl.num_programs(1) - 1)
    def _():
        x2 = x1_ref[...] + gt_ref[0] * acc_ref[...]
        if final_norm:
            x2 = _rms(x2) * fg_ref[...]
        o_ref[...] = x2


FFN_TM = 512
FFN_HALO = 2 * SUBLANES


def _ffn(h2, halo, w_up, conv_w, conv_b, w_down, x1, mod, final_g, *, n_rows, n_lat_tiles,
         seq, n_batch, ctx_len, final_norm):
    d = D_MODEL
    tm, tf = FFN_TM, 512
    nf = D_FF // tf
    tpb = seq // tm
    mod_row = lambda i, j: (jnp.minimum(i // tpb, n_batch) * 6 + 5, 0, 0)
    return pl.pallas_call(
        functools.partial(_ffn_kernel, n_lat_tiles=n_lat_tiles, ctx_len=ctx_len,
                          final_norm=final_norm),
        grid=(n_rows // tm, nf),
        in_specs=[pl.BlockSpec((tm, d), lambda i, j: (i, 0)),
                  pl.BlockSpec((1, FFN_HALO, d), lambda i, j: (i, 0, 0)),
                  pl.BlockSpec((d, tf), lambda i, j: (0, j)),
                  pl.BlockSpec((d, tf), lambda i, j: (0, nf + j)),
                  pl.BlockSpec((3, tf), lambda i, j: (0, j)),
                  pl.BlockSpec((3, tf), lambda i, j: (0, nf + j)),
                  pl.BlockSpec((1, tf), lambda i, j: (0, j)),
                  pl.BlockSpec((1, tf), lambda i, j: (0, nf + j)),
                  pl.BlockSpec((tf, d), lambda i, j: (j, 0)),
                  pl.BlockSpec((tm, d), lambda i, j: (i, 0)),
                  pl.BlockSpec((1, 1, d), mod_row),
                  pl.BlockSpec((1, d), lambda i, j: (0, 0))],
        out_specs=pl.BlockSpec((tm, d), lambda i, j: (i, 0)),
        out_shape=jax.ShapeDtypeStruct((n_rows, d), F32),
        scratch_shapes=[pltpu.VMEM((tm, d), F32), pltpu.VMEM((tm + FFN_HALO, d), BF)],
        compiler_params=_params(("parallel", "arbitrary")),
        name="ffn",
    )(h2, halo, w_up, w_up, conv_w, conv_w, conv_b, conv_b, w_down, x1, mod, final_g)


def _ffn_halo(h2, tm, n_tiles, n_lat, seq, ctx_len):
    d = h2.shape[1]
    starts = np.arange(n_tiles) * tm
    seq_of = np.where(starts < n_lat, seq, ctx_len)
    rel = np.where(starts < n_lat, starts, starts - n_lat)
    prev_ok = jnp.asarray((rel % seq_of != 0).astype(np.float32))[:, None]
    next_ok = jnp.asarray(((rel + tm) % seq_of != 0).astype(np.float32))[:, None]
    hh = h2[:n_tiles * tm]
    zero = jnp.zeros((1, d), h2.dtype)
    prev = jnp.concatenate([zero, hh[tm - 1::tm][:-1]], axis=0) * prev_ok.astype(h2.dtype)
    nxt = jnp.concatenate([hh[tm::tm], zero], axis=0) * next_ok.astype(h2.dtype)
    pad = jnp.zeros((n_tiles, FFN_HALO - 2, d), h2.dtype)
    return jnp.concatenate([prev[:, None], nxt[:, None], pad], axis=1)


def kernel(x, c, ctx, c_ctx, ada_w, ada_b, norm1_g, w_in, na_rpb, hg_lower_bounds, hg_norm_g,
           gla_gate_up, gla_gate_b, gla_norm_g, w_out, norm2_g, w_up, conv_w, conv_b, w_down, final_g):
    n_batch, seq, d = x.shape
    ctx_len = ctx.shape[1]
    depth = ada_w.shape[0]
    n_lat = n_batch * seq
    n_all = n_lat + n_batch * ctx_len
    assert d == D_MODEL and ctx_len == SCAN_CHUNK and seq % (NA_QROWS * GRID_W) == 0
    assert seq // GRID_W >= NA_KROWS + NA_QROWS and n_batch < SUBLANES

    x_all = jnp.concatenate([x.reshape(n_lat, d), ctx.reshape(n_batch * ctx_len, d)], axis=0)
    c_all = jnp.concatenate([c, c_ctx[None], jnp.zeros((SUBLANES - n_batch - 1, d), F32)], axis=0)
    mods = _ada(c_all, ada_w, ada_b)
    cos_t, sin_t = _rope_tables(seq, ctx_len)

    for l in range(depth):
        last = l == depth - 1
        mod = mods[l].reshape(SUBLANES * 6, 1, d)
        w_main = w_in[l, :, :MAIN_COLS].astype(BF)
        w_rank = jnp.pad(w_in[l, :, MAIN_COLS:], ((0, 0), (0, LANES - 2 * GLA_RANK))).astype(BF)
        p_na, p, r = _inproj(x_all, norm1_g[l][None], mod, w_main, w_rank, seq=seq, n_batch=n_batch)

        bias = _na_bias_table(na_rpb[l], seq // GRID_W)
        na_o = _na(p_na, bias, n_batch=n_batch, seq=seq, ctx_len=ctx_len, n_all=n_all)
        if not last:
            na_o = _cattn(p_na, na_o, n_batch=n_batch, seq=seq, ctx_len=ctx_len)

        o_hg = _hg_scan(p, hg_lower_bounds, layer=l, n_batch=n_batch, seq=seq, n_all=n_all)
        up = gla_gate_up[l]
        up_pad = jnp.zeros((2, LANES, GLA_KEY_WIDTH), F32)
        up_pad = up_pad.at[0, :GLA_RANK].set(up[0]).at[1, GLA_RANK:2 * GLA_RANK].set(up[1])
        o_gl = _gla_scan(p, r, up_pad, gla_gate_b[l][:, None, :], cos_t, sin_t,
                         n_batch=n_batch, seq=seq, n_all=n_all)

        n_rows = n_lat if last else n_all
        x1, h2 = _outproj(na_o, o_hg, o_gl, p, hg_norm_g[l][None], gla_norm_g[l][None],
                          w_out[l].astype(BF), x_all, mod, norm2_g[l][None], n_rows=n_rows,
                          seq=seq, n_batch=n_batch)
        n_tiles = n_rows // FFN_TM
        halo = _ffn_halo(h2, FFN_TM, n_tiles, n_lat, seq, ctx_len)
        x_all = _ffn(h2, halo, w_up[l].astype(BF), conv_w[l], conv_b[l][None], w_down[l].astype(BF),
                     x1, mod, final_g[None], n_rows=n_rows, n_lat_tiles=n_lat // FFN_TM,
                     seq=seq, n_batch=n_batch, ctx_len=ctx_len, final_norm=last)
    return x_all[:n_lat].reshape(n_batch, seq, d)
```

```python
import functools

import numpy as np
import jax
import jax.numpy as jnp
from jax import lax
from jax.experimental import pallas as pl
from jax.experimental.pallas import tpu as pltpu

F32 = jnp.float32
BF = jnp.bfloat16

D_MODEL = 2048
GRID_W = 64
NA_HEADS = 8
NA_HEAD_DIM = 128
NA_WIDTH = NA_HEADS * NA_HEAD_DIM
NA_ROWS = 8
NA_COLS = 16
HG_WIDTH = 512
HG_HEADS = 4
GLA_WIDTH = 512
GLA_HEADS = 4
GLA_KEY_WIDTH = 256
GLA_HEAD_K = 64
GLA_RANK = 16
GLA_GATE_NORM = 16.0
D_FF = 5632
ROPE_BASE = 10000.0
EPS = 1e-6
MAIN_COLS = 7168
NA_COLS_ALL = 3 * NA_WIDTH
COL_NA_Q, COL_NA_K, COL_NA_V = 0, 1024, 2048
COL_HG_Q, COL_HG_I, COL_HG_ZF, COL_HG_G = 0, 512, 1024, 2048
COL_GL_Q, COL_GL_K, COL_GL_V, COL_GL_G = 2560, 2816, 3072, 3584

LANES = 128
SUBLANES = 8
VMEM_LIMIT = 56 * 1024 * 1024

NA_QROWS = 8
NA_KROWS = 16
SCAN_CHUNK = 256
SCAN_BLOCK = 16
NEG = -1e30
LOG2E = 1.4426950408889634
SAFE_LOG2_SPAN = 96.0
HG_FAST_BLOCK = 32
GLA_FAST_BLOCK = SCAN_CHUNK


def _dot(a, b):
    return jnp.dot(a, b, preferred_element_type=F32)


def _dot_nt(a, b):
    return lax.dot_general(a, b, (((1,), (1,)), ((), ())), preferred_element_type=F32)


def _dot_tn(a, b):
    return lax.dot_general(a, b, (((0,), (0,)), ((), ())), preferred_element_type=F32)


def _sigmoid(x):
    return 1.0 / (1.0 + jnp.exp(-x))


def _rms(x):
    return x * lax.rsqrt(jnp.mean(x * x, axis=-1, keepdims=True) + EPS)


def _params(sem):
    return pltpu.CompilerParams(dimension_semantics=sem, vmem_limit_bytes=VMEM_LIMIT)


def _ada_kernel(c_ref, w_ref, b_ref, o_ref):
    c = c_ref[...]
    s = (c * _sigmoid(c)).astype(BF)
    o_ref[0] = _dot(s, w_ref[0].astype(BF)) + b_ref[0]


def _ada(c_all, ada_w, ada_b):
    depth, d, n = ada_w.shape
    tn = 1024
    return pl.pallas_call(
        _ada_kernel,
        grid=(depth, n // tn),
        in_specs=[pl.BlockSpec((SUBLANES, d), lambda l, j: (0, 0)),
                  pl.BlockSpec((1, d, tn), lambda l, j: (l, 0, j)),
                  pl.BlockSpec((1, 1, tn), lambda l, j: (l, 0, j))],
        out_specs=pl.BlockSpec((1, SUBLANES, tn), lambda l, j: (l, 0, j)),
        out_shape=jax.ShapeDtypeStruct((depth, SUBLANES, n), F32),
        compiler_params=_params(("parallel", "parallel")),
        name="ada",
    )(c_all, ada_w, ada_b.reshape(depth, 1, n))


def _inproj_kernel(x_ref, g_ref, sh_ref, sc_ref, w_ref, wr_ref, pna_ref, prest_ref, r_ref, h_ref,
                   *, n_na_tiles):
    j = pl.program_id(1)

    @pl.when(j == 0)
    def _():
        y = _rms(x_ref[...]) * g_ref[...]
        h = (y * (1.0 + sc_ref[0]) + sh_ref[0]).astype(BF)
        h_ref[...] = h
        r_ref[...] = _dot(h, wr_ref[...])

    acc = _dot(h_ref[...], w_ref[...])

    @pl.when(j < n_na_tiles)
    def _():
        pna_ref[...] = acc.astype(BF)

    @pl.when(j >= n_na_tiles)
    def _():
        prest_ref[...] = acc


def _inproj(x_all, norm_g, mod, w_main, w_rank, *, seq, n_batch):
    n, d = x_all.shape
    tm, tn = 1024, 1024
    n_na = NA_COLS_ALL // tn
    tpb = seq // tm
    mod_row = lambda k: (lambda i, j: (jnp.minimum(i // tpb, n_batch) * 6 + k, 0, 0))
    return pl.pallas_call(
        functools.partial(_inproj_kernel, n_na_tiles=n_na),
        grid=(n // tm, MAIN_COLS // tn),
        in_specs=[pl.BlockSpec((tm, d), lambda i, j: (i, 0)),
                  pl.BlockSpec((1, d), lambda i, j: (0, 0)),
                  pl.BlockSpec((1, 1, d), mod_row(0)),
                  pl.BlockSpec((1, 1, d), mod_row(1)),
                  pl.BlockSpec((d, tn), lambda i, j: (0, j)),
                  pl.BlockSpec((d, LANES), lambda i, j: (0, 0))],
        out_specs=[pl.BlockSpec((tm, tn), lambda i, j: (i, jnp.minimum(j, n_na - 1))),
                   pl.BlockSpec((tm, tn), lambda i, j: (i, jnp.maximum(j - n_na, 0))),
                   pl.BlockSpec((tm, LANES), lambda i, j: (i, 0))],
        out_shape=[jax.ShapeDtypeStruct((n, NA_COLS_ALL), BF),
                   jax.ShapeDtypeStruct((n, MAIN_COLS - NA_COLS_ALL), F32),
                   jax.ShapeDtypeStruct((n, LANES), F32)],
        scratch_shapes=[pltpu.VMEM((tm, d), BF)],
        compiler_params=_params(("parallel", "arbitrary")),
        name="inproj",
    )(x_all, norm_g, mod, mod, w_main, w_rank)


def _na_bias_table(rpb, rows):
    n_heads = rpb.shape[0]
    n_dr = 2 * NA_ROWS - 1
    qc = np.arange(GRID_W)[:, None]
    kc = np.arange(GRID_W)[None, :]
    cstart = np.clip(qc - NA_COLS // 2, 0, GRID_W - NA_COLS)
    col_ok = (kc >= cstart) & (kc < cstart + NA_COLS)
    dc_idx = np.clip(kc - qc, -(NA_COLS - 1), NA_COLS - 1) + NA_COLS - 1
    onehot = (dc_idx[None] == np.arange(2 * NA_COLS - 1)[:, None, None]).astype(np.float32)
    by_col = jnp.einsum('hrj,jqk->hqrk', rpb, jnp.asarray(onehot), precision=lax.Precision.HIGHEST)
    by_col = jnp.where(jnp.asarray(col_ok)[None, :, None, :], by_col, NEG)
    pad = NA_KROWS
    flat = jnp.pad(by_col, ((0, 0), (0, 0), (pad, pad), (0, 0))).reshape(n_heads, GRID_W, -1)
    kr = np.arange(NA_KROWS)
    cases = []
    for r0 in (0, NA_QROWS, rows - NA_QROWS):
        ks = int(np.clip(r0 - NA_ROWS // 2, 0, rows - NA_KROWS))
        per_row = []
        for qr in range(NA_QROWS):
            r = r0 + qr
            rs = int(np.clip(r - NA_ROWS // 2, 0, rows - NA_ROWS))
            row_ok = np.repeat((ks + kr >= rs) & (ks + kr < rs + NA_ROWS), GRID_W)
            lo = ks - r + NA_ROWS - 1 + pad
            assert 0 <= lo and lo + NA_KROWS <= n_dr + 2 * pad
            sl = flat[:, :, lo * GRID_W:(lo + NA_KROWS) * GRID_W]
            per_row.append(jnp.where(jnp.asarray(row_ok)[None, None, :], sl, NEG))
        cases.append(jnp.stack(per_row, axis=1).reshape(n_heads, NA_QROWS * GRID_W, NA_KROWS * GRID_W))
    return jnp.stack(cases, axis=1)


def _na_kernel(q_ref, k_ref, v_ref, kc_ref, vc_ref, bias_ref, o_ref, *, rows):
    rb = pl.program_id(2)
    nk = NA_KROWS * GRID_W
    ks = jnp.clip(rb * NA_QROWS - NA_ROWS // 2, 0, rows - NA_KROWS) * GRID_W
    ks = pl.multiple_of(ks, GRID_W)
    kblk = k_ref[pl.ds(ks, nk), :]
    vblk = v_ref[pl.ds(ks, nk), :]
    q = (q_ref[...].astype(F32) * (NA_HEAD_DIM ** -0.5)).astype(BF)
    s_loc = _dot_nt(q, kblk) + bias_ref[0, 0]
    s_ctx = _dot_nt(q, kc_ref[...])
    m = jnp.maximum(jnp.max(s_loc, axis=-1, keepdims=True), jnp.max(s_ctx, axis=-1, keepdims=True))
    p_loc = jnp.exp(s_loc - m)
    p_ctx = jnp.exp(s_ctx - m)
    l = jnp.sum(p_loc, axis=-1, keepdims=True) + jnp.sum(p_ctx, axis=-1, keepdims=True)
    o = _dot(p_loc.astype(BF), vblk) + _dot(p_ctx.astype(BF), vc_ref[...])
    o_ref[...] = (o / l).astype(o_ref.dtype)


def _na(p, bias, *, n_batch, seq, ctx_len, n_all):
    rows = seq // GRID_W
    tq = NA_QROWS * GRID_W
    rbs = seq // tq
    hd = NA_HEAD_DIM
    ctx_blk0 = (n_batch * seq) // ctx_len
    case = lambda rb: jnp.where(rb == 0, 0, jnp.where(rb == rbs - 1, 2, 1))
    return pl.pallas_call(
        functools.partial(_na_kernel, rows=rows),
        grid=(n_batch, NA_HEADS, rbs),
        in_specs=[pl.BlockSpec((tq, hd), lambda b, h, r: (b * rbs + r, COL_NA_Q // hd + h)),
                  pl.BlockSpec((seq, hd), lambda b, h, r: (b, COL_NA_K // hd + h)),
                  pl.BlockSpec((seq, hd), lambda b, h, r: (b, COL_NA_V // hd + h)),
                  pl.BlockSpec((ctx_len, hd), lambda b, h, r: (ctx_blk0 + b, COL_NA_K // hd + h)),
                  pl.BlockSpec((ctx_len, hd), lambda b, h, r: (ctx_blk0 + b, COL_NA_V // hd + h)),
                  pl.BlockSpec((1, 1, tq, NA_KROWS * GRID_W), lambda b, h, r: (h, case(r), 0, 0))],
        out_specs=pl.BlockSpec((tq, hd), lambda b, h, r: (b * rbs + r, h)),
        out_shape=jax.ShapeDtypeStruct((n_all, NA_WIDTH), BF),
        compiler_params=_params(("parallel", "parallel", "arbitrary")),
        name="na",
    )(p, p, p, p, p, bias)


def _cattn_kernel(q_ref, k_ref, v_ref, prev_ref, o_ref):
    del prev_ref
    q = (q_ref[...].astype(F32) * (NA_HEAD_DIM ** -0.5)).astype(BF)
    s = _dot_nt(q, k_ref[...])
    p = jnp.exp(s - jnp.max(s, axis=-1, keepdims=True))
    l = jnp.sum(p, axis=-1, keepdims=True)
    o_ref[...] = (_dot(p.astype(BF), v_ref[...]) / l).astype(o_ref.dtype)


def _cattn(p, na_o, *, n_batch, seq, ctx_len):
    hd = NA_HEAD_DIM
    blk0 = (n_batch * seq) // ctx_len
    return pl.pallas_call(
        _cattn_kernel,
        grid=(n_batch, NA_HEADS),
        in_specs=[pl.BlockSpec((ctx_len, hd), lambda b, h: (blk0 + b, COL_NA_Q // hd + h)),
                  pl.BlockSpec((ctx_len, hd), lambda b, h: (blk0 + b, COL_NA_K // hd + h)),
                  pl.BlockSpec((ctx_len, hd), lambda b, h: (blk0 + b, COL_NA_V // hd + h)),
                  pl.BlockSpec(memory_space=pl.ANY)],
        out_specs=pl.BlockSpec((ctx_len, hd), lambda b, h: (blk0 + b, h)),
        out_shape=jax.ShapeDtypeStruct(na_o.shape, na_o.dtype),
        input_output_aliases={3: 0},
        compiler_params=_params(("parallel", "parallel")),
        name="cattn",
    )(p, p, p, na_o)


def _bcast_rows(ref, idx_of_group, group, n_groups, lane0=0):
    return jnp.concatenate(
        [jnp.broadcast_to(ref[idx_of_group(g):idx_of_group(g) + 1, lane0:lane0 + LANES], (group, LANES))
         for g in range(n_groups)], axis=0)


def _masked(x, mask):
    return x if mask is None else x * mask


def _in_block_exact(q, k, cum, v_src, fwd, cum_ref, k_ref):
    cs = q.shape[0]
    nh = len(v_src)
    bs = SCAN_BLOCK
    nb = cs // bs
    n_half = bs // SUBLANES
    assert n_half == 2
    d_idx = lax.broadcasted_iota(jnp.int32, (LANES, LANES * nh), 0)
    j_idx = lax.broadcasted_iota(jnp.int32, (LANES, LANES * nh), 1)
    ones_mat = jnp.where(d_idx // (LANES // nh) == j_idx // LANES, 1.0, 0.0).astype(BF)

    def half(x, h):
        return jnp.concatenate([x[b * bs + h * SUBLANES:b * bs + (h + 1) * SUBLANES] for b in range(nb)],
                               axis=0)

    q_h = [half(q, h) for h in range(n_half)]
    c_h = [half(cum, h) for h in range(n_half)]
    tl = lax.broadcasted_iota(jnp.int32, (nb * SUBLANES, LANES), 0) & (SUBLANES - 1)
    acc = [[jnp.zeros((nb * SUBLANES, LANES), F32) for _ in range(n_half)] for _ in range(nh)]
    for s in range(bs):
        s_half, s_loc = divmod(s, SUBLANES)
        at = lambda b: b * bs + s
        c_s = _bcast_rows(cum_ref, at, SUBLANES, nb)
        k_s = _bcast_rows(k_ref, at, SUBLANES, nb)
        v_s = [_bcast_rows(ref, at, SUBLANES, nb, lane0) for ref, lane0 in v_src]
        halves = [h for h in range(n_half) if (h >= s_half if fwd else h <= s_half)]
        parts = []
        for h in halves:
            x = c_h[h] - c_s
            if h == s_half and s_loc != (0 if fwd else SUBLANES - 1):
                x = jnp.where(tl >= s_loc if fwd else tl <= s_loc, x, NEG)
            parts.append((q_h[h] * k_s * jnp.exp2(x)).astype(BF))
        r = _dot(jnp.concatenate(parts, axis=0), ones_mat)
        for i, h in enumerate(halves):
            rh = r[i * nb * SUBLANES:(i + 1) * nb * SUBLANES]
            for j in range(nh):
                acc[j][h] = acc[j][h] + rh[:, j * LANES:(j + 1) * LANES] * v_s[j]
    return [jnp.concatenate([acc[j][h][b * SUBLANES:(b + 1) * SUBLANES]
                             for b in range(nb) for h in range(n_half)], axis=0) for j in range(nh)]


def _in_block_factored(q, k, x0, vs, masks, fwd, block):
    cs = q.shape[0]
    row = lax.broadcasted_iota(jnp.int32, (cs, cs), 0)
    col = lax.broadcasted_iota(jnp.int32, (cs, cs), 1)
    ordered = row >= col if fwd else row <= col
    qf = q * jnp.exp2(x0)
    kf = (k * jnp.exp2(-x0)).astype(BF)
    outs = []
    for j in range(len(vs)):
        a = jnp.where(ordered, _dot_nt(_masked(qf, masks[j]).astype(BF), kf), 0.0)
        if block < cs:
            a = jnp.where((row ^ col) < block, a, 0.0)
        outs.append(_dot(a.astype(BF), vs[j].astype(BF)))
    return outs


def _levels(q, k, cum, vs, masks, fwd, cum_ref, m_lo, m_hi):
    cs = q.shape[0]
    nh = len(vs)
    bs = SCAN_BLOCK
    slabs = [[None] * (cs // bs) for _ in range(nh)]
    m = m_lo
    while m < m_hi:
        pair = 2 * m
        n_pairs = cs // pair
        late_off, early_off = (m, 0) if fwd else (0, m)
        bidx = m - 1 if fwd else m

        def take(x, off):
            return jnp.concatenate([x[p * pair + off:p * pair + off + m] for p in range(n_pairs)], axis=0)

        cb = _bcast_rows(cum_ref, lambda p: p * pair + bidx, m, n_pairs)
        qt = take(q, late_off) * jnp.exp2(take(cum, late_off) - cb)
        kt = (take(k, early_off) * jnp.exp2(cb - take(cum, early_off))).astype(BF)
        if n_pairs > 1:
            rh = lax.broadcasted_iota(jnp.int32, (cs // 2, cs // 2), 0)
            ch = lax.broadcasted_iota(jnp.int32, (cs // 2, cs // 2), 1)
            same_pair = (rh ^ ch) < m
        for j in range(nh):
            a = _dot_nt(_masked(qt, masks[j]).astype(BF), kt)
            if n_pairs > 1:
                a = jnp.where(same_pair, a, 0.0)
            o = _dot(a.astype(BF), take(vs[j], early_off).astype(BF))
            for p in range(n_pairs):
                for i in range(m // bs):
                    idx = (p * pair + late_off) // bs + i
                    piece = o[p * m + i * bs:p * m + (i + 1) * bs]
                    slabs[j][idx] = piece if slabs[j][idx] is None else slabs[j][idx] + piece
        m = pair
    return slabs


class _Chain:
    def __init__(self, scratch, c, nh):
        st_ref, cum_ref, k_ref, q_ref, rest_ref = scratch
        self.st = st_ref.at[c * nh:(c + 1) * nh]
        self.cum, self.k, self.q = cum_ref.at[c], k_ref.at[c], q_ref.at[c]
        self.rest = rest_ref.at[c * nh:(c + 1) * nh]


def _scan_chunk(q, k, lg, vs, v_src, masks, dirn, ch, write_out, fast_block):
    cs = q.shape[0]
    nh = len(vs)
    bs = SCAN_BLOCK
    fwd = dirn == 0
    row = lax.broadcasted_iota(jnp.int32, (cs, cs), 0)
    col = lax.broadcasted_iota(jnp.int32, (cs, cs), 1)
    tri = jnp.where(row >= col if fwd else row <= col, 1.0, 0.0).astype(BF)
    hi = lg.astype(BF)
    r1 = lg - hi.astype(F32)
    mid = r1.astype(BF)
    lo = (r1 - mid.astype(F32)).astype(BF)
    cum = (_dot(tri, hi) + _dot(tri, mid) + _dot(tri, lo)) * LOG2E

    ch.cum[...] = cum
    ch.k[...] = k
    ch.q[...] = q

    first = 0 if fwd else fast_block - 1
    x0 = cum - _bcast_rows(ch.cum, lambda b: b * fast_block + first, fast_block, cs // fast_block)
    span = jnp.max(-x0)
    in_block = _in_block_factored(q, k, x0, vs, masks, fwd, fast_block)

    slabs = _levels(q, k, cum, vs, masks, fwd, ch.cum, fast_block, cs)

    last_row = cs - 1 if fwd else 0
    last = ch.cum[last_row:last_row + 1, :]
    qh = q * jnp.exp2(cum)
    kh = k * jnp.exp2(last - cum)
    dec = jnp.exp2(last)
    outs = []
    for j in range(nh):
        st = ch.st[j]
        vj = vs[j].astype(BF)
        rest = _join_slabs(slabs[j]) + _dot_nt(_masked(qh, masks[j]).astype(BF), st.astype(BF))
        ch.st[j] = st * dec + _dot_tn(vj, _masked(kh, masks[j]).astype(BF))
        ch.rest[j] = rest
        outs.append(rest + in_block[j])
    write_out(outs)

    def fixup():
        @pl.when(jnp.logical_not(span <= SAFE_LOG2_SPAN))
        def _():
            q_, k_, cum_ = ch.q[...], ch.k[...], ch.cum[...]
            exact = _in_block_exact(q_, k_, cum_, v_src, fwd, ch.cum, ch.k)
            vs_ = [ref[:, lane0:lane0 + LANES] for ref, lane0 in v_src]
            low = _levels(q_, k_, cum_, vs_, masks, fwd, ch.cum, bs, fast_block)
            write_out([ch.rest[j] + exact[j] + _join_slabs(low[j]) for j in range(nh)])

    return fixup


def _join_slabs(slabs):
    if all(sl is None for sl in slabs):
        return 0.0
    return jnp.concatenate([jnp.zeros((SCAN_BLOCK, LANES), F32) if sl is None else sl for sl in slabs],
                           axis=0)


def _hg_kernel(qf_ref, vf_ref, zf_ref, qb_ref, vb_ref, zb_ref, lbp_ref, of_ref, ob_ref, *scratch, layer):
    @pl.when(pl.program_id(2) == 0)
    def _():
        scratch[0][...] = jnp.zeros_like(scratch[0])

    fixups = []
    for dirn, (q_ref, v_ref, z_ref, o_ref) in enumerate(((qf_ref, vf_ref, zf_ref, of_ref),
                                                         (qb_ref, vb_ref, zb_ref, ob_ref))):
        p = lbp_ref[dirn]
        e = jnp.exp(p - jnp.max(p, axis=0, keepdims=True))
        den = jnp.sum(e, axis=0, keepdims=True)
        lb_all = jnp.zeros_like(den)
        for i in range(1, layer + 1):
            lb_all = lb_all + e[i:i + 1]
        lb_all = lb_all / den
        for hh in range(HG_STEP_HEADS):
            lanes = slice(hh * LANES, (hh + 1) * LANES)
            lb = lb_all[:, lanes]
            z = z_ref[:, lanes]
            en = jnp.exp(-jnp.abs(z))
            big = 1.0 / (1.0 + en)
            small = en * big
            sig = jnp.where(z >= 0, big, small)
            nsig = jnp.where(z >= 0, small, big)
            k = (1.0 - lb) * nsig
            lg = jnp.log(lb + (1.0 - lb) * sig)
            qr = q_ref[:, lanes]
            q = qr * _sigmoid(qr)

            def write_out(outs, o_ref=o_ref, lanes=lanes):
                o_ref[:, lanes] = outs[0]

            ch = _Chain(scratch, dirn * HG_STEP_HEADS + hh, 1)
            fixups.append(_scan_chunk(q, k, lg, [v_ref[:, lanes]], [(v_ref, hh * LANES)], [None], dirn,
                                      ch, write_out, HG_FAST_BLOCK))
    for fixup in fixups:
        fixup()


def _gla_kernel(qf_ref, kf_ref, vf_ref, rf_ref, cosf_ref, sinf_ref,
                qb_ref, kb_ref, vb_ref, rb_ref, cosb_ref, sinb_ref, up_ref, gb_ref, of_ref, ob_ref, *scratch):
    @pl.when(pl.program_id(2) == 0)
    def _():
        scratch[0][...] = jnp.zeros_like(scratch[0])

    cs = qf_ref.shape[0]
    lane = lax.broadcasted_iota(jnp.int32, (cs, LANES), 1)
    first = (lane & 16) == 0
    lane1 = lax.broadcasted_iota(jnp.int32, (1, LANES), 1)
    masks = [jnp.where(lane1 < GLA_HEAD_K, 1.0, 0.0), jnp.where(lane1 >= GLA_HEAD_K, 1.0, 0.0)]
    fixups = []
    for dirn, (q_ref, k_ref, v_ref, r_ref, cos_ref, sin_ref, o_ref) in enumerate(
            ((qf_ref, kf_ref, vf_ref, rf_ref, cosf_ref, sinf_ref, of_ref),
             (qb_ref, kb_ref, vb_ref, rb_ref, cosb_ref, sinb_ref, ob_ref))):
        cos = cos_ref[...]
        sin = sin_ref[...]

        def rope(x):
            partner = jnp.where(first, pltpu.roll(x, LANES - 16, 1), pltpu.roll(x, 16, 1))
            return x * cos + partner * sin

        q = rope(q_ref[...] * (GLA_HEAD_K ** -0.5))
        k = rope(k_ref[...])
        pre = _dot(r_ref[...].astype(BF), up_ref[dirn].astype(BF)) + gb_ref[dirn]
        lg = (jnp.minimum(pre, 0.0) - jnp.log(1.0 + jnp.exp(-jnp.abs(pre)))) * (1.0 / GLA_GATE_NORM)

        def write_out(outs, o_ref=o_ref):
            o_ref[...] = jnp.concatenate(outs, axis=1)

        ch = _Chain(scratch, dirn, 2)
        fixups.append(_scan_chunk(q, k, lg, [v_ref[:, :LANES], v_ref[:, LANES:]],
                                  [(v_ref, 0), (v_ref, LANES)], masks, dirn, ch, write_out, GLA_FAST_BLOCK))
    for fixup in fixups:
        fixup()


HG_STEP_HEADS = 2


def _scan_scratch(n_chains, nh):
    cs = SCAN_CHUNK
    return [pltpu.VMEM((n_chains * nh, LANES, LANES), F32), pltpu.VMEM((n_chains, cs, LANES), F32),
            pltpu.VMEM((n_chains, cs, LANES), F32), pltpu.VMEM((n_chains, cs, LANES), F32),
            pltpu.VMEM((n_chains * nh, cs, LANES), F32)]


def _scan_rowblks(nc, ctx_blk0):
    fwd = lambda b, c: jnp.where(c == 0, ctx_blk0 + b, b * nc + c - 1)
    bwd = lambda b, c: jnp.where(c == 0, ctx_blk0 + b, b * nc + nc - c)
    return fwd, bwd


def _hg_scan(p, lbp, *, layer, n_batch, seq, n_all):
    cs = SCAN_CHUNK
    nc = seq // cs
    w = HG_STEP_HEADS * LANES
    rbs = _scan_rowblks(nc, (n_batch * seq) // cs)
    depth = lbp.shape[1]
    col = lambda c0: c0 // w

    def specs(rb, zcol):
        return [pl.BlockSpec((cs, w), lambda b, h, c: (rb(b, c), col(COL_HG_Q) + h)),
                pl.BlockSpec((cs, w), lambda b, h, c: (rb(b, c), col(COL_HG_I) + h)),
                pl.BlockSpec((cs, w), lambda b, h, c: (rb(b, c), col(zcol) + h))]

    return pl.pallas_call(
        functools.partial(_hg_kernel, layer=layer),
        grid=(n_batch, HG_HEADS // HG_STEP_HEADS, nc + 1),
        in_specs=specs(rbs[0], COL_HG_ZF) + specs(rbs[1], COL_HG_ZF + HG_WIDTH)
        + [pl.BlockSpec((2, depth, w), lambda b, h, c: (0, 0, h))],
        out_specs=[pl.BlockSpec((cs, w), lambda b, h, c: (rbs[0](b, c), h)),
                   pl.BlockSpec((cs, w), lambda b, h, c: (rbs[1](b, c), h))],
        out_shape=[jax.ShapeDtypeStruct((n_all, HG_WIDTH), F32)] * 2,
        scratch_shapes=_scan_scratch(2 * HG_STEP_HEADS, 1),
        compiler_params=_params(("parallel", "parallel", "arbitrary")),
        name="hg_scan",
    )(p, p, p, p, p, p, lbp)


def _gla_scan(p, r, up_pad, gate_b, cos_t, sin_t, *, n_batch, seq, n_all):
    cs = SCAN_CHUNK
    nc = seq // cs
    rbs = _scan_rowblks(nc, (n_batch * seq) // cs)
    tbs = (lambda b, c: jnp.where(c == 0, nc, c - 1), lambda b, c: jnp.where(c == 0, nc, nc - c))

    def specs(rb, tb):
        return [pl.BlockSpec((cs, LANES), lambda b, h, c: (rb(b, c), COL_GL_Q // LANES + h)),
                pl.BlockSpec((cs, LANES), lambda b, h, c: (rb(b, c), COL_GL_K // LANES + h)),
                pl.BlockSpec((cs, 2 * LANES), lambda b, h, c: (rb(b, c), COL_GL_V // (2 * LANES) + h)),
                pl.BlockSpec((cs, LANES), lambda b, h, c: (rb(b, c), 0)),
                pl.BlockSpec((cs, LANES), lambda b, h, c: (tb(b, c), 0)),
                pl.BlockSpec((cs, LANES), lambda b, h, c: (tb(b, c), 0))]

    return pl.pallas_call(
        _gla_kernel,
        grid=(n_batch, GLA_HEADS // 2, nc + 1),
        in_specs=specs(rbs[0], tbs[0]) + specs(rbs[1], tbs[1])
        + [pl.BlockSpec((2, LANES, LANES), lambda b, h, c: (0, 0, h)),
           pl.BlockSpec((2, 1, LANES), lambda b, h, c: (0, 0, h))],
        out_specs=[pl.BlockSpec((cs, 2 * LANES), lambda b, h, c: (rbs[0](b, c), h)),
                   pl.BlockSpec((cs, 2 * LANES), lambda b, h, c: (rbs[1](b, c), h))],
        out_shape=[jax.ShapeDtypeStruct((n_all, GLA_WIDTH), F32)] * 2,
        scratch_shapes=_scan_scratch(2, 2),
        compiler_params=_params(("parallel", "parallel", "arbitrary")),
        name="gla_scan",
    )(p, p, p, r, cos_t, sin_t, p, p, p, r, cos_t, sin_t, up_pad, gate_b)


def _rope_tables(seq, ctx_len):
    half = GLA_HEAD_K // 2
    inv = ROPE_BASE ** (-jnp.arange(0, half, 2, dtype=F32) / half)
    pos = jnp.arange(seq)
    ang_r = (pos // GRID_W).astype(F32)[:, None] * inv
    ang_c = (pos % GRID_W).astype(F32)[:, None] * inv
    cos = jnp.concatenate([jnp.cos(ang_r)] * 2 + [jnp.cos(ang_c)] * 2, axis=1)
    sin = jnp.concatenate([-jnp.sin(ang_r), jnp.sin(ang_r), -jnp.sin(ang_c), jnp.sin(ang_c)], axis=1)
    cos = jnp.concatenate([jnp.tile(cos, (1, 2)), jnp.ones((ctx_len, LANES), F32)], axis=0)
    sin = jnp.concatenate([jnp.tile(sin, (1, 2)), jnp.zeros((ctx_len, LANES), F32)], axis=0)
    return cos, sin


def _outproj_kernel(na_ref, hgf_ref, hgb_ref, ghg_ref, glf_ref, glb_ref, ggl_ref, nhg_ref, ngl_ref, w_ref, x_ref,
                    gt_ref, sh_ref, sc_ref, n2_ref, x1_ref, h2_ref, mix_ref):
    mix_ref[:, :NA_WIDTH] = na_ref[...]

    def norm_gate(of_ref, ob_ref, g_ref, gain_ref, col0, n_heads):
        o = of_ref[...] + ob_ref[...]
        g = g_ref[...]
        for h in range(n_heads):
            sl = slice(h * LANES, (h + 1) * LANES)
            gh = g[:, sl]
            y = _rms(o[:, sl]) * gain_ref[...]
            mix_ref[:, col0 + h * LANES:col0 + (h + 1) * LANES] = (y * (gh * _sigmoid(gh))).astype(BF)

    norm_gate(hgf_ref, hgb_ref, ghg_ref, nhg_ref, NA_WIDTH, HG_HEADS)
    norm_gate(glf_ref, glb_ref, ggl_ref, ngl_ref, NA_WIDTH + HG_WIDTH, GLA_HEADS)
    x1 = x_ref[...] + gt_ref[0] * _dot(mix_ref[...], w_ref[...])
    x1_ref[...] = x1
    y = _rms(x1) * n2_ref[...]
    h2_ref[...] = (y * (1.0 + sc_ref[0]) + sh_ref[0]).astype(BF)


def _outproj(na_o, o_hg, o_gl, p, hg_gain, gl_gain, w_out, x_all, mod, norm2_g, *,
             n_rows, seq, n_batch):
    d = D_MODEL
    tm = 256
    tpb = seq // tm
    mod_row = lambda k: (lambda i: (jnp.minimum(i // tpb, n_batch) * 6 + k, 0, 0))
    return pl.pallas_call(
        _outproj_kernel,
        grid=(n_rows // tm,),
        in_specs=[pl.BlockSpec((tm, NA_WIDTH), lambda i: (i, 0)),
                  pl.BlockSpec((tm, HG_WIDTH), lambda i: (i, 0)),
                  pl.BlockSpec((tm, HG_WIDTH), lambda i: (i, 0)),
                  pl.BlockSpec((tm, HG_WIDTH), lambda i: (i, COL_HG_G // HG_WIDTH)),
                  pl.BlockSpec((tm, GLA_WIDTH), lambda i: (i, 0)),
                  pl.BlockSpec((tm, GLA_WIDTH), lambda i: (i, 0)),
                  pl.BlockSpec((tm, GLA_WIDTH), lambda i: (i, COL_GL_G // GLA_WIDTH)),
                  pl.BlockSpec((1, LANES), lambda i: (0, 0)),
                  pl.BlockSpec((1, LANES), lambda i: (0, 0)),
                  pl.BlockSpec((d, d), lambda i: (0, 0)),
                  pl.BlockSpec((tm, d), lambda i: (i, 0)),
                  pl.BlockSpec((1, 1, d), mod_row(2)),
                  pl.BlockSpec((1, 1, d), mod_row(3)),
                  pl.BlockSpec((1, 1, d), mod_row(4)),
                  pl.BlockSpec((1, d), lambda i: (0, 0))],
        out_specs=[pl.BlockSpec((tm, d), lambda i: (i, 0)),
                   pl.BlockSpec((tm, d), lambda i: (i, 0))],
        out_shape=[jax.ShapeDtypeStruct((n_rows, d), F32),
                   jax.ShapeDtypeStruct((n_rows, d), BF)],
        scratch_shapes=[pltpu.VMEM((tm, d), BF)],
        compiler_params=_params(("parallel",)),
        name="outproj",
    )(na_o, o_hg[0], o_hg[1], p, o_gl[0], o_gl[1], p, hg_gain, gl_gain, w_out, x_all, mod, mod, mod, norm2_g)


def _patch_row(x, r, row_in_slab, value, keep=None):
    r0 = (r // SUBLANES) * SUBLANES
    slab = x[r0:r0 + SUBLANES]
    new = value if keep is None else slab * keep + value * (1.0 - keep)
    slab = jnp.where(row_in_slab == r - r0, new, slab)
    parts = [x[:r0]] * (r0 > 0) + [slab] + [x[r0 + SUBLANES:]] * (r0 + SUBLANES < x.shape[0])
    return jnp.concatenate(parts, axis=0)


def _ffn_kernel(h_ref, halo_ref, wa_ref, wb_ref, cwa_ref, cwb_ref, cba_ref, cbb_ref, wd_ref,
                x1_ref, gt_ref, fg_ref, o_ref, hc_ref, *, n_lat_tiles, ctx_len, final_norm):
    i = pl.program_id(0)
    j = pl.program_id(1)
    tm = h_ref.shape[0]
    tf = wa_ref.shape[1]

    @pl.when(j == 0)
    def _():
        o_ref[...] = jnp.zeros_like(o_ref)
        hc_ref[:tm] = h_ref[...]
        hc_ref[tm:] = halo_ref[0]

    hc = hc_ref[...]
    row8 = lax.broadcasted_iota(jnp.int32, (SUBLANES, tf), 0)
    keep_lat = jnp.where(i < n_lat_tiles, 1.0, 0.0)
    zero_row = jnp.zeros((1, tf), F32)

    def conv(w_ref, cw_ref, cb_ref):
        u_all = _dot(hc, w_ref[...])
        u = u_all[:tm]
        prev = _patch_row(pltpu.roll(u, 1, 0), 0, row8, u_all[tm:tm + 1])
        nxt = _patch_row(pltpu.roll(u, tm - 1, 0), tm - 1, row8, u_all[tm + 1:tm + 2])
        for r in range(ctx_len, tm, ctx_len):
            prev = _patch_row(prev, r, row8, zero_row, keep_lat)
            nxt = _patch_row(nxt, r - 1, row8, zero_row, keep_lat)
        cw = cw_ref[...]
        return cw[0:1] * prev + cw[1:2] * u + cw[2:3] * nxt + cb_ref[...]

    a = conv(wa_ref, cwa_ref, cba_ref)
    b = conv(wb_ref, cwb_ref, cbb_ref)
    g = (a * _sigmoid(a) * b).astype(BF)
    o_ref[...] += _dot(g, wd_ref[...])

    @pl.when(j == pl.num_programs(1) - 1)
    def _():
        x2 = x1_ref[...] + gt_ref[0] * o_ref[...]
        if final_norm:
            x2 = _rms(x2) * fg_ref[...]
        o_ref[...] = x2


FFN_TM = 1024
FFN_HALO = 2 * SUBLANES


def _ffn(h2, halo, w_up, conv_w, conv_b, w_down, x1, mod, final_g, *, n_rows, n_lat_tiles,
         seq, n_batch, ctx_len, final_norm):
    d = D_MODEL
    tm, tf = FFN_TM, 256
    nf = D_FF // tf
    tpb = seq // tm
    mod_row = lambda i, j: (jnp.minimum(i // tpb, n_batch) * 6 + 5, 0, 0)
    return pl.pallas_call(
        functools.partial(_ffn_kernel, n_lat_tiles=n_lat_tiles, ctx_len=ctx_len,
                          final_norm=final_norm),
        grid=(n_rows // tm, nf),
        in_specs=[pl.BlockSpec((tm, d), lambda i, j: (i, 0)),
                  pl.BlockSpec((1, FFN_HALO, d), lambda i, j: (i, 0, 0)),
                  pl.BlockSpec((d, tf), lambda i, j: (0, j)),
                  pl.BlockSpec((d, tf), lambda i, j: (0, nf + j)),
                  pl.BlockSpec((3, tf), lambda i, j: (0, j)),
                  pl.BlockSpec((3, tf), lambda i, j: (0, nf + j)),
                  pl.BlockSpec((1, tf), lambda i, j: (0, j)),
                  pl.BlockSpec((1, tf), lambda i, j: (0, nf + j)),
                  pl.BlockSpec((tf, d), lambda i, j: (j, 0)),
                  pl.BlockSpec((tm, d), lambda i, j: (i, 0), pipeline_mode=pl.Buffered(1)),
                  pl.BlockSpec((1, 1, d), mod_row),
                  pl.BlockSpec((1, d), lambda i, j: (0, 0))],
        out_specs=pl.BlockSpec((tm, d), lambda i, j: (i, 0)),
        out_shape=jax.ShapeDtypeStruct((n_rows, d), F32),
        scratch_shapes=[pltpu.VMEM((tm + FFN_HALO, d), BF)],
        compiler_params=_params(("parallel", "arbitrary")),
        name="ffn",
    )(h2, halo, w_up, w_up, conv_w, conv_w, conv_b, conv_b, w_down, x1, mod, final_g)


def _ffn_halo(h2, tm, n_tiles, n_lat, seq, ctx_len):
    d = h2.shape[1]
    starts = np.arange(n_tiles) * tm
    seq_of = np.where(starts < n_lat, seq, ctx_len)
    rel = np.where(starts < n_lat, starts, starts - n_lat)
    prev_ok = jnp.asarray((rel % seq_of != 0).astype(np.float32))[:, None]
    next_ok = jnp.asarray(((rel + tm) % seq_of != 0).astype(np.float32))[:, None]
    hh = h2[:n_tiles * tm]
    zero = jnp.zeros((1, d), h2.dtype)
    prev = jnp.concatenate([zero, hh[tm - 1::tm][:-1]], axis=0) * prev_ok.astype(h2.dtype)
    nxt = jnp.concatenate([hh[tm::tm], zero], axis=0) * next_ok.astype(h2.dtype)
    pad = jnp.zeros((n_tiles, FFN_HALO - 2, d), h2.dtype)
    return jnp.concatenate([prev[:, None], nxt[:, None], pad], axis=1)


def kernel(x, c, ctx, c_ctx, ada_w, ada_b, norm1_g, w_in, na_rpb, hg_lower_bounds, hg_norm_g,
           gla_gate_up, gla_gate_b, gla_norm_g, w_out, norm2_g, w_up, conv_w, conv_b, w_down, final_g):
    n_batch, seq, d = x.shape
    ctx_len = ctx.shape[1]
    depth = ada_w.shape[0]
    n_lat = n_batch * seq
    n_all = n_lat + n_batch * ctx_len
    assert d == D_MODEL and ctx_len == SCAN_CHUNK and seq % (NA_QROWS * GRID_W) == 0
    assert seq // GRID_W >= NA_KROWS + NA_QROWS and n_batch < SUBLANES

    x_all = jnp.concatenate([x.reshape(n_lat, d), ctx.reshape(n_batch * ctx_len, d)], axis=0)
    c_all = jnp.concatenate([c, c_ctx[None], jnp.zeros((SUBLANES - n_batch - 1, d), F32)], axis=0)
    mods = _ada(c_all, ada_w, ada_b)
    cos_t, sin_t = _rope_tables(seq, ctx_len)

    for l in range(depth):
        last = l == depth - 1
        mod = mods[l].reshape(SUBLANES * 6, 1, d)
        w_main = w_in[l, :, :MAIN_COLS].astype(BF)
        w_rank = jnp.pad(w_in[l, :, MAIN_COLS:], ((0, 0), (0, LANES - 2 * GLA_RANK))).astype(BF)
        p_na, p, r = _inproj(x_all, norm1_g[l][None], mod, w_main, w_rank, seq=seq, n_batch=n_batch)

        bias = _na_bias_table(na_rpb[l], seq // GRID_W)
        na_o = _na(p_na, bias, n_batch=n_batch, seq=seq, ctx_len=ctx_len, n_all=n_all)
        if not last:
            na_o = _cattn(p_na, na_o, n_batch=n_batch, seq=seq, ctx_len=ctx_len)

        o_hg = _hg_scan(p, hg_lower_bounds, layer=l, n_batch=n_batch, seq=seq, n_all=n_all)
        up = gla_gate_up[l]
        up_pad = jnp.zeros((2, LANES, GLA_KEY_WIDTH), F32)
        up_pad = up_pad.at[0, :GLA_RANK].set(up[0]).at[1, GLA_RANK:2 * GLA_RANK].set(up[1])
        o_gl = _gla_scan(p, r, up_pad, gla_gate_b[l][:, None, :], cos_t, sin_t,
                         n_batch=n_batch, seq=seq, n_all=n_all)

        n_rows = n_lat if last else n_all
        x1, h2 = _outproj(na_o, o_hg, o_gl, p, hg_norm_g[l][None], gla_norm_g[l][None],
                          w_out[l].astype(BF), x_all, mod, norm2_g[l][None], n_rows=n_rows,
                          seq=seq, n_batch=n_batch)
        n_tiles = n_rows // FFN_TM
        halo = _ffn_halo(h2, FFN_TM, n_tiles, n_lat, seq, ctx_len)
        x_all = _ffn(h2, halo, w_up[l].astype(BF), conv_w[l], conv_b[l][None], w_down[l].astype(BF),
                     x1, mod, final_g[None], n_rows=n_rows, n_lat_tiles=n_lat // FFN_TM,
                     seq=seq, n_batch=n_batch, ctx_len=ctx_len, final_norm=last)
    return x_all[:n_lat].reshape(n_batch, seq, d)
```

```python
import functools

import numpy as np
import jax
import jax.numpy as jnp
from jax import lax
from jax.experimental import pallas as pl
from jax.experimental.pallas import tpu as pltpu

F32 = jnp.float32
BF = jnp.bfloat16

D_MODEL = 2048
GRID_W = 64
NA_HEADS = 8
NA_HEAD_DIM = 128
NA_WIDTH = NA_HEADS * NA_HEAD_DIM
NA_ROWS = 8
NA_COLS = 16
HG_WIDTH = 512
HG_HEADS = 4
GLA_WIDTH = 512
GLA_HEADS = 4
GLA_KEY_WIDTH = 256
GLA_HEAD_K = 64
GLA_RANK = 16
GLA_GATE_NORM = 16.0
D_FF = 5632
ROPE_BASE = 10000.0
EPS = 1e-6
MAIN_COLS = 7168
NA_COLS_ALL = 3 * NA_WIDTH
COL_NA_Q, COL_NA_K, COL_NA_V = 0, 1024, 2048
COL_HG_Q, COL_HG_I, COL_HG_ZF, COL_HG_G = 0, 512, 1024, 2048
COL_GL_Q, COL_GL_K, COL_GL_V, COL_GL_G = 2560, 2816, 3072, 3584

LANES = 128
SUBLANES = 8
VMEM_LIMIT = 56 * 1024 * 1024

NA_QROWS = 8
NA_KROWS = 16
SCAN_CHUNK = 256
SCAN_BLOCK = 16
NEG = -1e30
LOG2E = 1.4426950408889634
SAFE_LOG2_SPAN = 96.0
HG_FAST_BLOCK = 16
GLA_FAST_BLOCK = SCAN_CHUNK


def _dot(a, b):
    return jnp.dot(a, b, preferred_element_type=F32)


def _dot_nt(a, b):
    return lax.dot_general(a, b, (((1,), (1,)), ((), ())), preferred_element_type=F32)


def _dot_tn(a, b):
    return lax.dot_general(a, b, (((0,), (0,)), ((), ())), preferred_element_type=F32)


def _sigmoid(x):
    return 1.0 / (1.0 + jnp.exp(-x))


def _rms(x):
    return x * lax.rsqrt(jnp.mean(x * x, axis=-1, keepdims=True) + EPS)


def _params(sem):
    return pltpu.CompilerParams(dimension_semantics=sem, vmem_limit_bytes=VMEM_LIMIT)


def _ada_kernel(c_ref, w_ref, b_ref, o_ref):
    c = c_ref[...]
    s = (c * _sigmoid(c)).astype(BF)
    o_ref[0] = _dot(s, w_ref[0].astype(BF)) + b_ref[0]


def _ada(c_all, ada_w, ada_b):
    depth, d, n = ada_w.shape
    tn = 1024
    return pl.pallas_call(
        _ada_kernel,
        grid=(depth, n // tn),
        in_specs=[pl.BlockSpec((SUBLANES, d), lambda l, j: (0, 0)),
                  pl.BlockSpec((1, d, tn), lambda l, j: (l, 0, j)),
                  pl.BlockSpec((1, 1, tn), lambda l, j: (l, 0, j))],
        out_specs=pl.BlockSpec((1, SUBLANES, tn), lambda l, j: (l, 0, j)),
        out_shape=jax.ShapeDtypeStruct((depth, SUBLANES, n), F32),
        compiler_params=_params(("parallel", "parallel")),
        name="ada",
    )(c_all, ada_w, ada_b.reshape(depth, 1, n))


def _inproj_kernel(x_ref, g_ref, sh_ref, sc_ref, w_ref, wr_ref, pna_ref, prest_ref, r_ref, h_ref,
                   *, n_na_tiles):
    j = pl.program_id(1)

    @pl.when(j == 0)
    def _():
        y = _rms(x_ref[...]) * g_ref[...]
        h = (y * (1.0 + sc_ref[0]) + sh_ref[0]).astype(BF)
        h_ref[...] = h
        r_ref[...] = _dot(h, wr_ref[...])

    acc = _dot(h_ref[...], w_ref[...])

    @pl.when(j < n_na_tiles)
    def _():
        pna_ref[...] = acc.astype(BF)

    @pl.when(j >= n_na_tiles)
    def _():
        prest_ref[...] = acc


def _inproj(x_all, norm_g, mod, w_main, w_rank, *, seq, n_batch):
    n, d = x_all.shape
    tm, tn = 1024, 1024
    n_na = NA_COLS_ALL // tn
    tpb = seq // tm
    mod_row = lambda k: (lambda i, j: (jnp.minimum(i // tpb, n_batch) * 6 + k, 0, 0))
    return pl.pallas_call(
        functools.partial(_inproj_kernel, n_na_tiles=n_na),
        grid=(n // tm, MAIN_COLS // tn),
        in_specs=[pl.BlockSpec((tm, d), lambda i, j: (i, 0)),
                  pl.BlockSpec((1, d), lambda i, j: (0, 0)),
                  pl.BlockSpec((1, 1, d), mod_row(0)),
                  pl.BlockSpec((1, 1, d), mod_row(1)),
                  pl.BlockSpec((d, tn), lambda i, j: (0, j)),
                  pl.BlockSpec((d, LANES), lambda i, j: (0, 0))],
        out_specs=[pl.BlockSpec((tm, tn), lambda i, j: (i, jnp.minimum(j, n_na - 1))),
                   pl.BlockSpec((tm, tn), lambda i, j: (i, jnp.maximum(j - n_na, 0))),
                   pl.BlockSpec((tm, LANES), lambda i, j: (i, 0))],
        out_shape=[jax.ShapeDtypeStruct((n, NA_COLS_ALL), BF),
                   jax.ShapeDtypeStruct((n, MAIN_COLS - NA_COLS_ALL), F32),
                   jax.ShapeDtypeStruct((n, LANES), F32)],
        scratch_shapes=[pltpu.VMEM((tm, d), BF)],
        compiler_params=_params(("parallel", "arbitrary")),
        name="inproj",
    )(x_all, norm_g, mod, mod, w_main, w_rank)


def _na_bias_table(rpb, rows):
    n_heads = rpb.shape[0]
    n_dr = 2 * NA_ROWS - 1
    qc = np.arange(GRID_W)[:, None]
    kc = np.arange(GRID_W)[None, :]
    cstart = np.clip(qc - NA_COLS // 2, 0, GRID_W - NA_COLS)
    col_ok = (kc >= cstart) & (kc < cstart + NA_COLS)
    dc_idx = np.clip(kc - qc, -(NA_COLS - 1), NA_COLS - 1) + NA_COLS - 1
    onehot = (dc_idx[None] == np.arange(2 * NA_COLS - 1)[:, None, None]).astype(np.float32)
    by_col = jnp.einsum('hrj,jqk->hqrk', rpb, jnp.asarray(onehot), precision=lax.Precision.HIGHEST)
    by_col = jnp.where(jnp.asarray(col_ok)[None, :, None, :], by_col, NEG)
    pad = NA_KROWS
    flat = jnp.pad(by_col, ((0, 0), (0, 0), (pad, pad), (0, 0))).reshape(n_heads, GRID_W, -1)
    kr = np.arange(NA_KROWS)
    cases = []
    for r0 in (0, NA_QROWS, rows - NA_QROWS):
        ks = int(np.clip(r0 - NA_ROWS // 2, 0, rows - NA_KROWS))
        per_row = []
        for qr in range(NA_QROWS):
            r = r0 + qr
            rs = int(np.clip(r - NA_ROWS // 2, 0, rows - NA_ROWS))
            row_ok = np.repeat((ks + kr >= rs) & (ks + kr < rs + NA_ROWS), GRID_W)
            lo = ks - r + NA_ROWS - 1 + pad
            assert 0 <= lo and lo + NA_KROWS <= n_dr + 2 * pad
            sl = flat[:, :, lo * GRID_W:(lo + NA_KROWS) * GRID_W]
            per_row.append(jnp.where(jnp.asarray(row_ok)[None, None, :], sl, NEG))
        cases.append(jnp.stack(per_row, axis=1).reshape(n_heads, NA_QROWS * GRID_W, NA_KROWS * GRID_W))
    return jnp.stack(cases, axis=1)


def _na_kernel(q_ref, k_ref, v_ref, kc_ref, vc_ref, bias_ref, o_ref, *, rows):
    rb = pl.program_id(2)
    nk = NA_KROWS * GRID_W
    ks = jnp.clip(rb * NA_QROWS - NA_ROWS // 2, 0, rows - NA_KROWS) * GRID_W
    ks = pl.multiple_of(ks, GRID_W)
    kblk = k_ref[pl.ds(ks, nk), :]
    vblk = v_ref[pl.ds(ks, nk), :]
    q = (q_ref[...].astype(F32) * (NA_HEAD_DIM ** -0.5)).astype(BF)
    s_loc = _dot_nt(q, kblk) + bias_ref[0, 0]
    s_ctx = _dot_nt(q, kc_ref[...])
    m = jnp.maximum(jnp.max(s_loc, axis=-1, keepdims=True), jnp.max(s_ctx, axis=-1, keepdims=True))
    p_loc = jnp.exp(s_loc - m)
    p_ctx = jnp.exp(s_ctx - m)
    l = jnp.sum(p_loc, axis=-1, keepdims=True) + jnp.sum(p_ctx, axis=-1, keepdims=True)
    o = _dot(p_loc.astype(BF), vblk) + _dot(p_ctx.astype(BF), vc_ref[...])
    o_ref[...] = (o / l).astype(o_ref.dtype)


def _na(p, bias, *, n_batch, seq, ctx_len):
    rows = seq // GRID_W
    tq = NA_QROWS * GRID_W
    rbs = seq // tq
    hd = NA_HEAD_DIM
    ctx_blk0 = (n_batch * seq) // ctx_len
    case = lambda rb: jnp.where(rb == 0, 0, jnp.where(rb == rbs - 1, 2, 1))
    return pl.pallas_call(
        functools.partial(_na_kernel, rows=rows),
        grid=(n_batch, NA_HEADS, rbs),
        in_specs=[pl.BlockSpec((tq, hd), lambda b, h, r: (b * rbs + r, COL_NA_Q // hd + h)),
                  pl.BlockSpec((seq, hd), lambda b, h, r: (b, COL_NA_K // hd + h)),
                  pl.BlockSpec((seq, hd), lambda b, h, r: (b, COL_NA_V // hd + h)),
                  pl.BlockSpec((ctx_len, hd), lambda b, h, r: (ctx_blk0 + b, COL_NA_K // hd + h)),
                  pl.BlockSpec((ctx_len, hd), lambda b, h, r: (ctx_blk0 + b, COL_NA_V // hd + h)),
                  pl.BlockSpec((1, 1, tq, NA_KROWS * GRID_W), lambda b, h, r: (h, case(r), 0, 0))],
        out_specs=pl.BlockSpec((tq, hd), lambda b, h, r: (b * rbs + r, h)),
        out_shape=jax.ShapeDtypeStruct((n_batch * seq, NA_WIDTH), BF),
        compiler_params=_params(("parallel", "parallel", "arbitrary")),
        name="na",
    )(p, p, p, p, p, bias)


def _cattn_kernel(q_ref, k_ref, v_ref, o_ref):
    q = (q_ref[...].astype(F32) * (NA_HEAD_DIM ** -0.5)).astype(BF)
    s = _dot_nt(q, k_ref[...])
    p = jnp.exp(s - jnp.max(s, axis=-1, keepdims=True))
    l = jnp.sum(p, axis=-1, keepdims=True)
    o_ref[...] = (_dot(p.astype(BF), v_ref[...]) / l).astype(o_ref.dtype)


def _cattn(p, *, n_batch, seq, ctx_len):
    hd = NA_HEAD_DIM
    blk0 = (n_batch * seq) // ctx_len
    return pl.pallas_call(
        _cattn_kernel,
        grid=(n_batch, NA_HEADS),
        in_specs=[pl.BlockSpec((ctx_len, hd), lambda b, h: (blk0 + b, COL_NA_Q // hd + h)),
                  pl.BlockSpec((ctx_len, hd), lambda b, h: (blk0 + b, COL_NA_K // hd + h)),
                  pl.BlockSpec((ctx_len, hd), lambda b, h: (blk0 + b, COL_NA_V // hd + h))],
        out_specs=pl.BlockSpec((ctx_len, hd), lambda b, h: (b, h)),
        out_shape=jax.ShapeDtypeStruct((n_batch * ctx_len, NA_WIDTH), BF),
        compiler_params=_params(("parallel", "parallel")),
        name="cattn",
    )(p, p, p)


def _bcast_rows(ref, idx_of_group, group, n_groups, lane0=0):
    return jnp.concatenate(
        [jnp.broadcast_to(ref[idx_of_group(g):idx_of_group(g) + 1, lane0:lane0 + LANES], (group, LANES))
         for g in range(n_groups)], axis=0)


def _masked(x, mask):
    return x if mask is None else x * mask


def _in_block_exact(q, k, cum, v_src, fwd, cum_ref, k_ref):
    cs = q.shape[0]
    nh = len(v_src)
    bs = SCAN_BLOCK
    nb = cs // bs
    n_half = bs // SUBLANES
    assert n_half == 2
    d_idx = lax.broadcasted_iota(jnp.int32, (LANES, LANES * nh), 0)
    j_idx = lax.broadcasted_iota(jnp.int32, (LANES, LANES * nh), 1)
    ones_mat = jnp.where(d_idx // (LANES // nh) == j_idx // LANES, 1.0, 0.0).astype(BF)

    def half(x, h):
        return jnp.concatenate([x[b * bs + h * SUBLANES:b * bs + (h + 1) * SUBLANES] for b in range(nb)],
                               axis=0)

    q_h = [half(q, h) for h in range(n_half)]
    c_h = [half(cum, h) for h in range(n_half)]
    tl = lax.broadcasted_iota(jnp.int32, (nb * SUBLANES, LANES), 0) & (SUBLANES - 1)
    acc = [[jnp.zeros((nb * SUBLANES, LANES), F32) for _ in range(n_half)] for _ in range(nh)]
    for s in range(bs):
        s_half, s_loc = divmod(s, SUBLANES)
        at = lambda b: b * bs + s
        c_s = _bcast_rows(cum_ref, at, SUBLANES, nb)
        k_s = _bcast_rows(k_ref, at, SUBLANES, nb)
        v_s = [_bcast_rows(ref, at, SUBLANES, nb, lane0) for ref, lane0 in v_src]
        halves = [h for h in range(n_half) if (h >= s_half if fwd else h <= s_half)]
        parts = []
        for h in halves:
            x = c_h[h] - c_s
            if h == s_half and s_loc != (0 if fwd else SUBLANES - 1):
                x = jnp.where(tl >= s_loc if fwd else tl <= s_loc, x, NEG)
            parts.append((q_h[h] * k_s * jnp.exp2(x)).astype(BF))
        r = _dot(jnp.concatenate(parts, axis=0), ones_mat)
        for i, h in enumerate(halves):
            rh = r[i * nb * SUBLANES:(i + 1) * nb * SUBLANES]
            for j in range(nh):
                acc[j][h] = acc[j][h] + rh[:, j * LANES:(j + 1) * LANES] * v_s[j]
    return [jnp.concatenate([acc[j][h][b * SUBLANES:(b + 1) * SUBLANES]
                             for b in range(nb) for h in range(n_half)], axis=0) for j in range(nh)]


def _in_block_factored(q, k, x0, vs, masks, fwd, block):
    cs = q.shape[0]
    row = lax.broadcasted_iota(jnp.int32, (cs, cs), 0)
    col = lax.broadcasted_iota(jnp.int32, (cs, cs), 1)
    ordered = row >= col if fwd else row <= col
    qf = q * jnp.exp2(x0)
    kf = (k * jnp.exp2(-x0)).astype(BF)
    outs = []
    for j in range(len(vs)):
        a = jnp.where(ordered, _dot_nt(_masked(qf, masks[j]).astype(BF), kf), 0.0)
        if block < cs:
            a = jnp.where((row ^ col) < block, a, 0.0)
        outs.append(_dot(a.astype(BF), vs[j].astype(BF)))
    return outs


def _levels(q, k, cum, vs, masks, fwd, cum_ref, m_lo, m_hi):
    cs = q.shape[0]
    nh = len(vs)
    bs = SCAN_BLOCK
    slabs = [[None] * (cs // bs) for _ in range(nh)]
    m = m_lo
    while m < m_hi:
        pair = 2 * m
        n_pairs = cs // pair
        late_off, early_off = (m, 0) if fwd else (0, m)
        bidx = m - 1 if fwd else m

        def take(x, off):
            return jnp.concatenate([x[p * pair + off:p * pair + off + m] for p in range(n_pairs)], axis=0)

        cb = _bcast_rows(cum_ref, lambda p: p * pair + bidx, m, n_pairs)
        qt = take(q, late_off) * jnp.exp2(take(cum, late_off) - cb)
        kt = (take(k, early_off) * jnp.exp2(cb - take(cum, early_off))).astype(BF)
        if n_pairs > 1:
            rh = lax.broadcasted_iota(jnp.int32, (cs // 2, cs // 2), 0)
            ch = lax.broadcasted_iota(jnp.int32, (cs // 2, cs // 2), 1)
            same_pair = (rh ^ ch) < m
        for j in range(nh):
            a = _dot_nt(_masked(qt, masks[j]).astype(BF), kt)
            if n_pairs > 1:
                a = jnp.where(same_pair, a, 0.0)
            o = _dot(a.astype(BF), take(vs[j], early_off).astype(BF))
            for p in range(n_pairs):
                for i in range(m // bs):
                    idx = (p * pair + late_off) // bs + i
                    piece = o[p * m + i * bs:p * m + (i + 1) * bs]
                    slabs[j][idx] = piece if slabs[j][idx] is None else slabs[j][idx] + piece
        m = pair
    return slabs


class _Chain:
    def __init__(self, scratch, c, nh):
        st_ref, cum_ref, k_ref, q_ref, rest_ref = scratch
        self.st = st_ref.at[c * nh:(c + 1) * nh]
        self.cum, self.k, self.q = cum_ref.at[c], k_ref.at[c], q_ref.at[c]
        self.rest = rest_ref.at[c * nh:(c + 1) * nh]


def _scan_chunk(q, k, lg, vs, v_src, masks, dirn, ch, write_out, fast_block):
    cs = q.shape[0]
    nh = len(vs)
    bs = SCAN_BLOCK
    fwd = dirn == 0
    row = lax.broadcasted_iota(jnp.int32, (cs, cs), 0)
    col = lax.broadcasted_iota(jnp.int32, (cs, cs), 1)
    tri = jnp.where(row >= col if fwd else row <= col, 1.0, 0.0).astype(BF)
    hi = lg.astype(BF)
    r1 = lg - hi.astype(F32)
    mid = r1.astype(BF)
    lo = (r1 - mid.astype(F32)).astype(BF)
    cum = (_dot(tri, hi) + _dot(tri, mid) + _dot(tri, lo)) * LOG2E

    ch.cum[...] = cum
    ch.k[...] = k
    ch.q[...] = q

    first = 0 if fwd else fast_block - 1
    x0 = cum - _bcast_rows(ch.cum, lambda b: b * fast_block + first, fast_block, cs // fast_block)
    span = jnp.max(-x0)
    in_block = _in_block_factored(q, k, x0, vs, masks, fwd, fast_block)

    slabs = _levels(q, k, cum, vs, masks, fwd, ch.cum, fast_block, cs)

    last_row = cs - 1 if fwd else 0
    last = ch.cum[last_row:last_row + 1, :]
    qh = q * jnp.exp2(cum)
    kh = k * jnp.exp2(last - cum)
    dec = jnp.exp2(last)
    outs = []
    for j in range(nh):
        st = ch.st[j]
        vj = vs[j].astype(BF)
        rest = _join_slabs(slabs[j]) + _dot_nt(_masked(qh, masks[j]).astype(BF), st.astype(BF))
        ch.st[j] = st * dec + _dot_tn(vj, _masked(kh, masks[j]).astype(BF))
        ch.rest[j] = rest
        outs.append(rest + in_block[j])
    write_out(outs)

    def fixup():
        @pl.when(jnp.logical_not(span <= SAFE_LOG2_SPAN))
        def _():
            q_, k_, cum_ = ch.q[...], ch.k[...], ch.cum[...]
            exact = _in_block_exact(q_, k_, cum_, v_src, fwd, ch.cum, ch.k)
            vs_ = [ref[:, lane0:lane0 + LANES] for ref, lane0 in v_src]
            low = _levels(q_, k_, cum_, vs_, masks, fwd, ch.cum, bs, fast_block)
            write_out([ch.rest[j] + exact[j] + _join_slabs(low[j]) for j in range(nh)])

    return fixup


def _join_slabs(slabs):
    if all(sl is None for sl in slabs):
        return 0.0
    return jnp.concatenate([jnp.zeros((SCAN_BLOCK, LANES), F32) if sl is None else sl for sl in slabs],
                           axis=0)


def _hg_kernel(qf_ref, vf_ref, zf_ref, qb_ref, vb_ref, zb_ref, lbp_ref, of_ref, ob_ref, *scratch, layer):
    @pl.when(pl.program_id(2) == 0)
    def _():
        scratch[0][...] = jnp.zeros_like(scratch[0])

    fixups = []
    for dirn, (q_ref, v_ref, z_ref, o_ref) in enumerate(((qf_ref, vf_ref, zf_ref, of_ref),
                                                         (qb_ref, vb_ref, zb_ref, ob_ref))):
        p = lbp_ref[dirn]
        e = jnp.exp(p - jnp.max(p, axis=0, keepdims=True))
        den = jnp.sum(e, axis=0, keepdims=True)
        lb_all = jnp.zeros_like(den)
        for i in range(1, layer + 1):
            lb_all = lb_all + e[i:i + 1]
        lb_all = lb_all / den
        for hh in range(HG_STEP_HEADS):
            lanes = slice(hh * LANES, (hh + 1) * LANES)
            lb = lb_all[:, lanes]
            z = z_ref[:, lanes]
            en = jnp.exp(-jnp.abs(z))
            big = 1.0 / (1.0 + en)
            small = en * big
            sig = jnp.where(z >= 0, big, small)
            nsig = jnp.where(z >= 0, small, big)
            k = (1.0 - lb) * nsig
            lg = jnp.log(lb + (1.0 - lb) * sig)
            qr = q_ref[:, lanes]
            q = qr * _sigmoid(qr)

            def write_out(outs, o_ref=o_ref, lanes=lanes):
                o_ref[:, lanes] = outs[0]

            ch = _Chain(scratch, dirn * HG_STEP_HEADS + hh, 1)
            fixups.append(_scan_chunk(q, k, lg, [v_ref[:, lanes]], [(v_ref, hh * LANES)], [None], dirn,
                                      ch, write_out, HG_FAST_BLOCK))
    for fixup in fixups:
        fixup()


def _gla_kernel(qf_ref, kf_ref, vf_ref, rf_ref, cosf_ref, sinf_ref,
                qb_ref, kb_ref, vb_ref, rb_ref, cosb_ref, sinb_ref, up_ref, gb_ref, of_ref, ob_ref, *scratch):
    @pl.when(pl.program_id(2) == 0)
    def _():
        scratch[0][...] = jnp.zeros_like(scratch[0])

    cs = qf_ref.shape[0]
    lane = lax.broadcasted_iota(jnp.int32, (cs, LANES), 1)
    first = (lane & 16) == 0
    lane1 = lax.broadcasted_iota(jnp.int32, (1, LANES), 1)
    masks = [jnp.where(lane1 < GLA_HEAD_K, 1.0, 0.0), jnp.where(lane1 >= GLA_HEAD_K, 1.0, 0.0)]
    fixups = []
    for dirn, (q_ref, k_ref, v_ref, r_ref, cos_ref, sin_ref, o_ref) in enumerate(
            ((qf_ref, kf_ref, vf_ref, rf_ref, cosf_ref, sinf_ref, of_ref),
             (qb_ref, kb_ref, vb_ref, rb_ref, cosb_ref, sinb_ref, ob_ref))):
        cos = cos_ref[...]
        sin = sin_ref[...]

        def rope(x):
            partner = jnp.where(first, pltpu.roll(x, LANES - 16, 1), pltpu.roll(x, 16, 1))
            return x * cos + partner * sin

        q = rope(q_ref[...] * (GLA_HEAD_K ** -0.5))
        k = rope(k_ref[...])
        pre = _dot(r_ref[...].astype(BF), up_ref[dirn].astype(BF)) + gb_ref[dirn]
        lg = (jnp.minimum(pre, 0.0) - jnp.log(1.0 + jnp.exp(-jnp.abs(pre)))) * (1.0 / GLA_GATE_NORM)

        def write_out(outs, o_ref=o_ref):
            o_ref[...] = jnp.concatenate(outs, axis=1)

        ch = _Chain(scratch, dirn, 2)
        fixups.append(_scan_chunk(q, k, lg, [v_ref[:, :LANES], v_ref[:, LANES:]],
                                  [(v_ref, 0), (v_ref, LANES)], masks, dirn, ch, write_out, GLA_FAST_BLOCK))
    for fixup in fixups:
        fixup()


HG_STEP_HEADS = 2


def _scan_scratch(n_chains, nh):
    cs = SCAN_CHUNK
    return [pltpu.VMEM((n_chains * nh, LANES, LANES), F32), pltpu.VMEM((n_chains, cs, LANES), F32),
            pltpu.VMEM((n_chains, cs, LANES), F32), pltpu.VMEM((n_chains, cs, LANES), F32),
            pltpu.VMEM((n_chains * nh, cs, LANES), F32)]


def _scan_rowblks(nc, ctx_blk0):
    fwd = lambda b, c: jnp.where(c == 0, ctx_blk0 + b, b * nc + c - 1)
    bwd = lambda b, c: jnp.where(c == 0, ctx_blk0 + b, b * nc + nc - c)
    return fwd, bwd


def _hg_scan(p, lbp, *, layer, n_batch, seq, n_all):
    cs = SCAN_CHUNK
    nc = seq // cs
    w = HG_STEP_HEADS * LANES
    rbs = _scan_rowblks(nc, (n_batch * seq) // cs)
    depth = lbp.shape[1]
    col = lambda c0: c0 // w

    def specs(rb, zcol):
        return [pl.BlockSpec((cs, w), lambda b, h, c: (rb(b, c), col(COL_HG_Q) + h)),
                pl.BlockSpec((cs, w), lambda b, h, c: (rb(b, c), col(COL_HG_I) + h)),
                pl.BlockSpec((cs, w), lambda b, h, c: (rb(b, c), col(zcol) + h))]

    return pl.pallas_call(
        functools.partial(_hg_kernel, layer=layer),
        grid=(n_batch, HG_HEADS // HG_STEP_HEADS, nc + 1),
        in_specs=specs(rbs[0], COL_HG_ZF) + specs(rbs[1], COL_HG_ZF + HG_WIDTH)
        + [pl.BlockSpec((2, depth, w), lambda b, h, c: (0, 0, h))],
        out_specs=[pl.BlockSpec((cs, w), lambda b, h, c: (rbs[0](b, c), h)),
                   pl.BlockSpec((cs, w), lambda b, h, c: (rbs[1](b, c), h))],
        out_shape=[jax.ShapeDtypeStruct((n_all, HG_WIDTH), F32)] * 2,
        scratch_shapes=_scan_scratch(2 * HG_STEP_HEADS, 1),
        compiler_params=_params(("parallel", "parallel", "arbitrary")),
        name="hg_scan",
    )(p, p, p, p, p, p, lbp)


def _gla_scan(p, r, up_pad, gate_b, cos_t, sin_t, *, n_batch, seq, n_all):
    cs = SCAN_CHUNK
    nc = seq // cs
    rbs = _scan_rowblks(nc, (n_batch * seq) // cs)
    tbs = (lambda b, c: jnp.where(c == 0, nc, c - 1), lambda b, c: jnp.where(c == 0, nc, nc - c))

    def specs(rb, tb):
        return [pl.BlockSpec((cs, LANES), lambda b, h, c: (rb(b, c), COL_GL_Q // LANES + h)),
                pl.BlockSpec((cs, LANES), lambda b, h, c: (rb(b, c), COL_GL_K // LANES + h)),
                pl.BlockSpec((cs, 2 * LANES), lambda b, h, c: (rb(b, c), COL_GL_V // (2 * LANES) + h)),
                pl.BlockSpec((cs, LANES), lambda b, h, c: (rb(b, c), 0)),
                pl.BlockSpec((cs, LANES), lambda b, h, c: (tb(b, c), 0)),
                pl.BlockSpec((cs, LANES), lambda b, h, c: (tb(b, c), 0))]

    return pl.pallas_call(
        _gla_kernel,
        grid=(n_batch, GLA_HEADS // 2, nc + 1),
        in_specs=specs(rbs[0], tbs[0]) + specs(rbs[1], tbs[1])
        + [pl.BlockSpec((2, LANES, LANES), lambda b, h, c: (0, 0, h)),
           pl.BlockSpec((2, 1, LANES), lambda b, h, c: (0, 0, h))],
        out_specs=[pl.BlockSpec((cs, 2 * LANES), lambda b, h, c: (rbs[0](b, c), h)),
                   pl.BlockSpec((cs, 2 * LANES), lambda b, h, c: (rbs[1](b, c), h))],
        out_shape=[jax.ShapeDtypeStruct((n_all, GLA_WIDTH), F32)] * 2,
        scratch_shapes=_scan_scratch(2, 2),
        compiler_params=_params(("parallel", "parallel", "arbitrary")),
        name="gla_scan",
    )(p, p, p, r, cos_t, sin_t, p, p, p, r, cos_t, sin_t, up_pad, gate_b)


def _rope_tables(seq, ctx_len):
    half = GLA_HEAD_K // 2
    inv = ROPE_BASE ** (-jnp.arange(0, half, 2, dtype=F32) / half)
    pos = jnp.arange(seq)
    ang_r = (pos // GRID_W).astype(F32)[:, None] * inv
    ang_c = (pos % GRID_W).astype(F32)[:, None] * inv
    cos = jnp.concatenate([jnp.cos(ang_r)] * 2 + [jnp.cos(ang_c)] * 2, axis=1)
    sin = jnp.concatenate([-jnp.sin(ang_r), jnp.sin(ang_r), -jnp.sin(ang_c), jnp.sin(ang_c)], axis=1)
    cos = jnp.concatenate([jnp.tile(cos, (1, 2)), jnp.ones((ctx_len, LANES), F32)], axis=0)
    sin = jnp.concatenate([jnp.tile(sin, (1, 2)), jnp.zeros((ctx_len, LANES), F32)], axis=0)
    return cos, sin


def _outproj_kernel(na_ref, nc_ref, hgf_ref, hgb_ref, ghg_ref, glf_ref, glb_ref, ggl_ref, nhg_ref, ngl_ref,
                    w_ref, x_ref, gt_ref, sh_ref, sc_ref, n2_ref, x1_ref, h2_ref, mix_ref, *, n_lat_tiles):
    @pl.when(pl.program_id(0) < n_lat_tiles)
    def _():
        mix_ref[:, :NA_WIDTH] = na_ref[...]

    @pl.when(pl.program_id(0) >= n_lat_tiles)
    def _():
        mix_ref[:, :NA_WIDTH] = nc_ref[...]

    def norm_gate(of_ref, ob_ref, g_ref, gain_ref, col0, n_heads):
        o = of_ref[...] + ob_ref[...]
        g = g_ref[...]
        for h in range(n_heads):
            sl = slice(h * LANES, (h + 1) * LANES)
            gh = g[:, sl]
            y = _rms(o[:, sl]) * gain_ref[...]
            mix_ref[:, col0 + h * LANES:col0 + (h + 1) * LANES] = (y * (gh * _sigmoid(gh))).astype(BF)

    norm_gate(hgf_ref, hgb_ref, ghg_ref, nhg_ref, NA_WIDTH, HG_HEADS)
    norm_gate(glf_ref, glb_ref, ggl_ref, ngl_ref, NA_WIDTH + HG_WIDTH, GLA_HEADS)
    x1 = x_ref[...] + gt_ref[0] * _dot(mix_ref[...], w_ref[...])
    x1_ref[...] = x1
    y = _rms(x1) * n2_ref[...]
    h2_ref[...] = (y * (1.0 + sc_ref[0]) + sh_ref[0]).astype(BF)


def _outproj(na_o, na_ctx, o_hg, o_gl, p, hg_gain, gl_gain, w_out, x_all, mod, norm2_g, *,
             n_rows, seq, n_batch):
    d = D_MODEL
    tm = 256
    tpb = seq // tm
    n_lat_tiles = n_batch * tpb
    mod_row = lambda k: (lambda i: (jnp.minimum(i // tpb, n_batch) * 6 + k, 0, 0))
    return pl.pallas_call(
        functools.partial(_outproj_kernel, n_lat_tiles=n_lat_tiles),
        grid=(n_rows // tm,),
        in_specs=[pl.BlockSpec((tm, NA_WIDTH), lambda i: (jnp.minimum(i, n_lat_tiles - 1), 0)),
                  pl.BlockSpec((tm, NA_WIDTH), lambda i: (jnp.maximum(i - n_lat_tiles, 0), 0)),
                  pl.BlockSpec((tm, HG_WIDTH), lambda i: (i, 0)),
                  pl.BlockSpec((tm, HG_WIDTH), lambda i: (i, 0)),
                  pl.BlockSpec((tm, HG_WIDTH), lambda i: (i, COL_HG_G // HG_WIDTH)),
                  pl.BlockSpec((tm, GLA_WIDTH), lambda i: (i, 0)),
                  pl.BlockSpec((tm, GLA_WIDTH), lambda i: (i, 0)),
                  pl.BlockSpec((tm, GLA_WIDTH), lambda i: (i, COL_GL_G // GLA_WIDTH)),
                  pl.BlockSpec((1, LANES), lambda i: (0, 0)),
                  pl.BlockSpec((1, LANES), lambda i: (0, 0)),
                  pl.BlockSpec((d, d), lambda i: (0, 0)),
                  pl.BlockSpec((tm, d), lambda i: (i, 0)),
                  pl.BlockSpec((1, 1, d), mod_row(2)),
                  pl.BlockSpec((1, 1, d), mod_row(3)),
                  pl.BlockSpec((1, 1, d), mod_row(4)),
                  pl.BlockSpec((1, d), lambda i: (0, 0))],
        out_specs=[pl.BlockSpec((tm, d), lambda i: (i, 0)),
                   pl.BlockSpec((tm, d), lambda i: (i, 0))],
        out_shape=[jax.ShapeDtypeStruct((n_rows, d), F32),
                   jax.ShapeDtypeStruct((n_rows, d), BF)],
        scratch_shapes=[pltpu.VMEM((tm, d), BF)],
        compiler_params=_params(("parallel",)),
        name="outproj",
    )(na_o, na_ctx, o_hg[0], o_hg[1], p, o_gl[0], o_gl[1], p, hg_gain, gl_gain, w_out, x_all, mod, mod, mod,
      norm2_g)


def _patch_row(x, r, row_in_slab, value, keep=None):
    r0 = (r // SUBLANES) * SUBLANES
    slab = x[r0:r0 + SUBLANES]
    new = value if keep is None else slab * keep + value * (1.0 - keep)
    slab = jnp.where(row_in_slab == r - r0, new, slab)
    parts = [x[:r0]] * (r0 > 0) + [slab] + [x[r0 + SUBLANES:]] * (r0 + SUBLANES < x.shape[0])
    return jnp.concatenate(parts, axis=0)


def _ffn_kernel(h_ref, halo_ref, wa_ref, wb_ref, cwa_ref, cwb_ref, cba_ref, cbb_ref, wd_ref,
                x1_ref, gt_ref, fg_ref, o_ref, hc_ref, u0_ref, u1_ref, *, n_lat_tiles, ctx_len,
                final_norm):
    i = pl.program_id(0)
    j = pl.program_id(1)
    nf = pl.num_programs(1) - 1
    tm = h_ref.shape[0]
    tf = wa_ref.shape[1]
    u_refs = (u0_ref, u1_ref)
    n_chunks = tf // FFN_CHUNK
    cols = [slice(c * FFN_CHUNK, (c + 1) * FFN_CHUNK) for c in range(n_chunks)]

    def up_pieces(u_ref):
        def piece(which, w_ref, sl):
            def run():
                res = _dot(hc_ref[...], w_ref[:, sl])
                u_ref[which, SUBLANES:SUBLANES + tm, sl] = res[:tm]
                u_ref[which, SUBLANES - 1:SUBLANES, sl] = res[tm:tm + 1]
                u_ref[which, SUBLANES + tm:SUBLANES + tm + 1, sl] = res[tm + 1:tm + 2]
            return run
        return [piece(which, w_ref, sl) for which, w_ref in ((0, wa_ref), (1, wb_ref)) for sl in cols]

    def down_pieces(u_ref):
        row8 = lax.broadcasted_iota(jnp.int32, (SUBLANES, FFN_CHUNK), 0)
        keep_lat = jnp.where(i < n_lat_tiles, 1.0, 0.0)
        zero_row = jnp.zeros((1, FFN_CHUNK), F32)
        gates = [None] * n_chunks

        def conv(which, cw_ref, cb_ref, sl):
            prev = u_ref[which, SUBLANES - 1:SUBLANES - 1 + tm, sl]
            u = u_ref[which, SUBLANES:SUBLANES + tm, sl]
            nxt = u_ref[which, SUBLANES + 1:SUBLANES + 1 + tm, sl]
            for r in range(ctx_len, tm, ctx_len):
                prev = _patch_row(prev, r, row8, zero_row, keep_lat)
                nxt = _patch_row(nxt, r - 1, row8, zero_row, keep_lat)
            cw = cw_ref[:, sl]
            return cw[0:1] * prev + cw[1:2] * u + cw[2:3] * nxt + cb_ref[:, sl]

        def gate(c):
            def run():
                a = conv(0, cwa_ref, cba_ref, cols[c])
                b = conv(1, cwb_ref, cbb_ref, cols[c])
                gates[c] = (a * _sigmoid(a) * b).astype(BF)
            return run

        def project(c):
            def run():
                o_ref[...] += _dot(gates[c], wd_ref[cols[c], :])
            return run

        return [gate(c) for c in range(n_chunks)] + [project(c) for c in range(n_chunks)]

    def run_all(pieces):
        for p in pieces:
            p()

    @pl.when(j == 0)
    def _():
        o_ref[...] = jnp.zeros_like(o_ref)
        hc_ref[:tm] = h_ref[...]
        hc_ref[tm:] = halo_ref[0]
        run_all(up_pieces(u_refs[0]))

    for parity in range(2):
        @pl.when(jnp.logical_and(jnp.logical_and(j > 0, j < nf), j % 2 == parity))
        def _(parity=parity):
            ups, downs = up_pieces(u_refs[parity]), down_pieces(u_refs[1 - parity])
            assert len(ups) == len(downs)
            for u_piece, d_piece in zip(ups, downs):
                u_piece()
                d_piece()

    @pl.when(j == nf)
    def _():
        run_all(down_pieces(u_refs[(D_FF // tf - 1) % 2]))
        x2 = x1_ref[...] + gt_ref[0] * o_ref[...]
        if final_norm:
            x2 = _rms(x2) * fg_ref[...]
        o_ref[...] = x2


FFN_TM = 512
FFN_HALO = 2 * SUBLANES
FFN_CHUNK = 256


def _ffn(h2, halo, w_up, conv_w, conv_b, w_down, x1, mod, final_g, *, n_rows, n_lat_tiles,
         seq, n_batch, ctx_len, final_norm):
    d = D_MODEL
    tm, tf = FFN_TM, 512
    nf = D_FF // tf
    tpb = seq // tm
    mod_row = lambda i, j: (jnp.minimum(i // tpb, n_batch) * 6 + 5, 0, 0)
    up_j = lambda j: jnp.minimum(j, nf - 1)
    dn_j = lambda j: jnp.maximum(j - 1, 0)
    return pl.pallas_call(
        functools.partial(_ffn_kernel, n_lat_tiles=n_lat_tiles, ctx_len=ctx_len,
                          final_norm=final_norm),
        grid=(n_rows // tm, nf + 1),
        in_specs=[pl.BlockSpec((tm, d), lambda i, j: (i, 0)),
                  pl.BlockSpec((1, FFN_HALO, d), lambda i, j: (i, 0, 0)),
                  pl.BlockSpec((d, tf), lambda i, j: (0, up_j(j))),
                  pl.BlockSpec((d, tf), lambda i, j: (0, nf + up_j(j))),
                  pl.BlockSpec((3, tf), lambda i, j: (0, dn_j(j))),
                  pl.BlockSpec((3, tf), lambda i, j: (0, nf + dn_j(j))),
                  pl.BlockSpec((1, tf), lambda i, j: (0, dn_j(j))),
                  pl.BlockSpec((1, tf), lambda i, j: (0, nf + dn_j(j))),
                  pl.BlockSpec((tf, d), lambda i, j: (dn_j(j), 0)),
                  pl.BlockSpec((tm, d), lambda i, j: (i, 0)),
                  pl.BlockSpec((1, 1, d), mod_row),
                  pl.BlockSpec((1, d), lambda i, j: (0, 0))],
        out_specs=pl.BlockSpec((tm, d), lambda i, j: (i, 0)),
        out_shape=jax.ShapeDtypeStruct((n_rows, d), F32),
        scratch_shapes=[pltpu.VMEM((tm + FFN_HALO, d), BF),
                        pltpu.VMEM((2, tm + FFN_HALO, tf), F32),
                        pltpu.VMEM((2, tm + FFN_HALO, tf), F32)],
        compiler_params=_params(("parallel", "arbitrary")),
        name="ffn",
    )(h2, halo, w_up, w_up, conv_w, conv_w, conv_b, conv_b, w_down, x1, mod, final_g)


def _ffn_halo(h2, tm, n_tiles, n_lat, seq, ctx_len):
    d = h2.shape[1]
    starts = np.arange(n_tiles) * tm
    seq_of = np.where(starts < n_lat, seq, ctx_len)
    rel = np.where(starts < n_lat, starts, starts - n_lat)
    prev_ok = jnp.asarray((rel % seq_of != 0).astype(np.float32))[:, None]
    next_ok = jnp.asarray(((rel + tm) % seq_of != 0).astype(np.float32))[:, None]
    hh = h2[:n_tiles * tm]
    zero = jnp.zeros((1, d), h2.dtype)
    prev = jnp.concatenate([zero, hh[tm - 1::tm][:-1]], axis=0) * prev_ok.astype(h2.dtype)
    nxt = jnp.concatenate([hh[tm::tm], zero], axis=0) * next_ok.astype(h2.dtype)
    pad = jnp.zeros((n_tiles, FFN_HALO - 2, d), h2.dtype)
    return jnp.concatenate([prev[:, None], nxt[:, None], pad], axis=1)


def kernel(x, c, ctx, c_ctx, ada_w, ada_b, norm1_g, w_in, na_rpb, hg_lower_bounds, hg_norm_g,
           gla_gate_up, gla_gate_b, gla_norm_g, w_out, norm2_g, w_up, conv_w, conv_b, w_down, final_g):
    n_batch, seq, d = x.shape
    ctx_len = ctx.shape[1]
    depth = ada_w.shape[0]
    n_lat = n_batch * seq
    n_all = n_lat + n_batch * ctx_len
    assert d == D_MODEL and ctx_len == SCAN_CHUNK and seq % (NA_QROWS * GRID_W) == 0
    assert seq // GRID_W >= NA_KROWS + NA_QROWS and n_batch < SUBLANES

    x_all = jnp.concatenate([x.reshape(n_lat, d), ctx.reshape(n_batch * ctx_len, d)], axis=0)
    c_all = jnp.concatenate([c, c_ctx[None], jnp.zeros((SUBLANES - n_batch - 1, d), F32)], axis=0)
    mods = _ada(c_all, ada_w, ada_b)
    cos_t, sin_t = _rope_tables(seq, ctx_len)

    for l in range(depth):
        last = l == depth - 1
        mod = mods[l].reshape(SUBLANES * 6, 1, d)
        w_main = w_in[l, :, :MAIN_COLS].astype(BF)
        w_rank = jnp.pad(w_in[l, :, MAIN_COLS:], ((0, 0), (0, LANES - 2 * GLA_RANK))).astype(BF)
        p_na, p, r = _inproj(x_all, norm1_g[l][None], mod, w_main, w_rank, seq=seq, n_batch=n_batch)

        bias = _na_bias_table(na_rpb[l], seq // GRID_W)
        na_o = _na(p_na, bias, n_batch=n_batch, seq=seq, ctx_len=ctx_len)
        na_ctx = na_o if last else _cattn(p_na, n_batch=n_batch, seq=seq, ctx_len=ctx_len)

        o_hg = _hg_scan(p, hg_lower_bounds, layer=l, n_batch=n_batch, seq=seq, n_all=n_all)
        up = gla_gate_up[l]
        up_pad = jnp.zeros((2, LANES, GLA_KEY_WIDTH), F32)
        up_pad = up_pad.at[0, :GLA_RANK].set(up[0]).at[1, GLA_RANK:2 * GLA_RANK].set(up[1])
        o_gl = _gla_scan(p, r, up_pad, gla_gate_b[l][:, None, :], cos_t, sin_t,
                         n_batch=n_batch, seq=seq, n_all=n_all)

        n_rows = n_lat if last else n_all
        x1, h2 = _outproj(na_o, na_ctx, o_hg, o_gl, p, hg_norm_g[l][None], gla_norm_g[l][None],
                          w_out[l].astype(BF), x_all, mod, norm2_g[l][None], n_rows=n_rows,
                          seq=seq, n_batch=n_batch)
        n_tiles = n_rows // FFN_TM
        halo = _ffn_halo(h2, FFN_TM, n_tiles, n_lat, seq, ctx_len)
        x_all = _ffn(h2, halo, w_up[l].astype(BF), conv_w[l], conv_b[l][None], w_down[l].astype(BF),
                     x1, mod, final_g[None], n_rows=n_rows, n_lat_tiles=n_lat // FFN_TM,
                     seq=seq, n_batch=n_batch, ctx_len=ctx_len, final_norm=last)
    return x_all[:n_lat].reshape(n_batch, seq, d)
```

```python
import functools

import numpy as np
import jax
import jax.numpy as jnp
from jax import lax
from jax.experimental import pallas as pl
from jax.experimental.pallas import tpu as pltpu

F32 = jnp.float32
BF = jnp.bfloat16

D_MODEL = 2048
GRID_W = 64
NA_HEADS = 8
NA_HEAD_DIM = 128
NA_WIDTH = NA_HEADS * NA_HEAD_DIM
NA_ROWS = 8
NA_COLS = 16
HG_WIDTH = 512
HG_HEADS = 4
GLA_WIDTH = 512
GLA_HEADS = 4
GLA_KEY_WIDTH = 256
GLA_HEAD_K = 64
GLA_RANK = 16
GLA_GATE_NORM = 16.0
D_FF = 5632
ROPE_BASE = 10000.0
EPS = 1e-6
MAIN_COLS = 7168
NA_COLS_ALL = 3 * NA_WIDTH
COL_NA_Q, COL_NA_K, COL_NA_V = 0, 1024, 2048
COL_HG_Q, COL_HG_I, COL_HG_ZF, COL_HG_G = 0, 512, 1024, 2048
COL_GL_Q, COL_GL_K, COL_GL_V, COL_GL_G = 2560, 2816, 3072, 3584

LANES = 128
SUBLANES = 8
VMEM_LIMIT = 56 * 1024 * 1024

NA_QROWS = 8
NA_KROWS = 16
SCAN_CHUNK = 256
SCAN_BLOCK = 16
NEG = -1e30
LOG2E = 1.4426950408889634
SAFE_LOG2_SPAN = 96.0
HG_FAST_BLOCK = 32
HG_MID_BLOCK = 16
GLA_FAST_BLOCK = SCAN_CHUNK


def _dot(a, b):
    return jnp.dot(a, b, preferred_element_type=F32)


def _dot_nt(a, b):
    return lax.dot_general(a, b, (((1,), (1,)), ((), ())), preferred_element_type=F32)


def _dot_tn(a, b):
    return lax.dot_general(a, b, (((0,), (0,)), ((), ())), preferred_element_type=F32)


def _sigmoid(x):
    return 1.0 / (1.0 + jnp.exp(-x))


def _rms(x):
    return x * lax.rsqrt(jnp.mean(x * x, axis=-1, keepdims=True) + EPS)


def _params(sem):
    return pltpu.CompilerParams(dimension_semantics=sem, vmem_limit_bytes=VMEM_LIMIT)


def _ada_kernel(c_ref, w_ref, b_ref, o_ref):
    c = c_ref[...]
    s = (c * _sigmoid(c)).astype(BF)
    o_ref[0] = _dot(s, w_ref[0].astype(BF)) + b_ref[0]


def _ada(c_all, ada_w, ada_b):
    depth, d, n = ada_w.shape
    tn = 1024
    return pl.pallas_call(
        _ada_kernel,
        grid=(depth, n // tn),
        in_specs=[pl.BlockSpec((SUBLANES, d), lambda l, j: (0, 0)),
                  pl.BlockSpec((1, d, tn), lambda l, j: (l, 0, j)),
                  pl.BlockSpec((1, 1, tn), lambda l, j: (l, 0, j))],
        out_specs=pl.BlockSpec((1, SUBLANES, tn), lambda l, j: (l, 0, j)),
        out_shape=jax.ShapeDtypeStruct((depth, SUBLANES, n), F32),
        compiler_params=_params(("parallel", "parallel")),
        name="ada",
    )(c_all, ada_w, ada_b.reshape(depth, 1, n))


def _inproj_kernel(x_ref, g_ref, sh_ref, sc_ref, w_ref, wr_ref, pna_ref, prest_ref, r_ref, h_ref,
                   *, n_na_tiles):
    j = pl.program_id(1)

    @pl.when(j == 0)
    def _():
        y = _rms(x_ref[...]) * g_ref[...]
        h = (y * (1.0 + sc_ref[0]) + sh_ref[0]).astype(BF)
        h_ref[...] = h
        r_ref[...] = _dot(h, wr_ref[...])

    acc = _dot(h_ref[...], w_ref[0])

    @pl.when(j < n_na_tiles)
    def _():
        pna_ref[...] = acc.astype(BF)

    @pl.when(j >= n_na_tiles)
    def _():
        prest_ref[...] = acc


def _inproj(x_all, norm_g, mod, w_in, w_rank, *, layer, seq, n_batch):
    n, d = x_all.shape
    tm, tn = 1024, 1024
    n_na = NA_COLS_ALL // tn
    tpb = seq // tm
    mod_row = lambda k: (lambda i, j: (jnp.minimum(i // tpb, n_batch) * 6 + k, 0, 0))
    return pl.pallas_call(
        functools.partial(_inproj_kernel, n_na_tiles=n_na),
        grid=(n // tm, MAIN_COLS // tn),
        in_specs=[pl.BlockSpec((tm, d), lambda i, j: (i, 0)),
                  pl.BlockSpec((1, d), lambda i, j: (0, 0)),
                  pl.BlockSpec((1, 1, d), mod_row(0)),
                  pl.BlockSpec((1, 1, d), mod_row(1)),
                  pl.BlockSpec((1, d, tn), lambda i, j: (layer, 0, j)),
                  pl.BlockSpec((d, LANES), lambda i, j: (0, 0))],
        out_specs=[pl.BlockSpec((tm, tn), lambda i, j: (i, jnp.minimum(j, n_na - 1))),
                   pl.BlockSpec((tm, tn), lambda i, j: (i, jnp.maximum(j - n_na, 0))),
                   pl.BlockSpec((tm, LANES), lambda i, j: (i, 0))],
        out_shape=[jax.ShapeDtypeStruct((n, NA_COLS_ALL), BF),
                   jax.ShapeDtypeStruct((n, MAIN_COLS - NA_COLS_ALL), F32),
                   jax.ShapeDtypeStruct((n, LANES), F32)],
        scratch_shapes=[pltpu.VMEM((tm, d), BF)],
        compiler_params=_params(("parallel", "arbitrary")),
        name="inproj",
    )(x_all, norm_g, mod, mod, w_in, w_rank)


def _na_bias_table(rpb, rows):
    n_heads = rpb.shape[0]
    n_dr = 2 * NA_ROWS - 1
    qc = np.arange(GRID_W)[:, None]
    kc = np.arange(GRID_W)[None, :]
    cstart = np.clip(qc - NA_COLS // 2, 0, GRID_W - NA_COLS)
    col_ok = (kc >= cstart) & (kc < cstart + NA_COLS)
    dc_idx = np.clip(kc - qc, -(NA_COLS - 1), NA_COLS - 1) + NA_COLS - 1
    onehot = (dc_idx[None] == np.arange(2 * NA_COLS - 1)[:, None, None]).astype(np.float32)
    by_col = jnp.einsum('hrj,jqk->hqrk', rpb, jnp.asarray(onehot), precision=lax.Precision.HIGHEST)
    by_col = jnp.where(jnp.asarray(col_ok)[None, :, None, :], by_col, NEG)
    pad = NA_KROWS
    flat = jnp.pad(by_col, ((0, 0), (0, 0), (pad, pad), (0, 0))).reshape(n_heads, GRID_W, -1)
    kr = np.arange(NA_KROWS)
    cases = []
    for r0 in (0, NA_QROWS, rows - NA_QROWS):
        ks = int(np.clip(r0 - NA_ROWS // 2, 0, rows - NA_KROWS))
        per_row = []
        for qr in range(NA_QROWS):
            r = r0 + qr
            rs = int(np.clip(r - NA_ROWS // 2, 0, rows - NA_ROWS))
            row_ok = np.repeat((ks + kr >= rs) & (ks + kr < rs + NA_ROWS), GRID_W)
            lo = ks - r + NA_ROWS - 1 + pad
            assert 0 <= lo and lo + NA_KROWS <= n_dr + 2 * pad
            sl = flat[:, :, lo * GRID_W:(lo + NA_KROWS) * GRID_W]
            per_row.append(jnp.where(jnp.asarray(row_ok)[None, None, :], sl, NEG))
        cases.append(jnp.stack(per_row, axis=1).reshape(n_heads, NA_QROWS * GRID_W, NA_KROWS * GRID_W))
    return jnp.stack(cases, axis=1)


def _na_kernel(q_ref, k_ref, v_ref, kc_ref, vc_ref, bias_ref, o_ref, *, rows):
    rb = pl.program_id(2)
    nk = NA_KROWS * GRID_W
    ks = jnp.clip(rb * NA_QROWS - NA_ROWS // 2, 0, rows - NA_KROWS) * GRID_W
    ks = pl.multiple_of(ks, GRID_W)
    for hh in range(NA_STEP_HEADS):
        lanes = slice(hh * NA_HEAD_DIM, (hh + 1) * NA_HEAD_DIM)
        kblk = k_ref[pl.ds(ks, nk), lanes]
        vblk = v_ref[pl.ds(ks, nk), lanes]
        q = (q_ref[:, lanes].astype(F32) * (NA_HEAD_DIM ** -0.5)).astype(BF)
        s_loc = _dot_nt(q, kblk) + bias_ref[hh, 0]
        s_ctx = _dot_nt(q, kc_ref[:, lanes])
        m = jnp.maximum(jnp.max(s_loc, axis=-1, keepdims=True), jnp.max(s_ctx, axis=-1, keepdims=True))
        p_loc = jnp.exp(s_loc - m)
        p_ctx = jnp.exp(s_ctx - m)
        l = jnp.sum(p_loc, axis=-1, keepdims=True) + jnp.sum(p_ctx, axis=-1, keepdims=True)
        o = _dot(p_loc.astype(BF), vblk) + _dot(p_ctx.astype(BF), vc_ref[:, lanes])
        o_ref[:, lanes] = (o / l).astype(o_ref.dtype)


NA_STEP_HEADS = 2


def _na(p, bias, *, n_batch, seq, ctx_len):
    rows = seq // GRID_W
    tq = NA_QROWS * GRID_W
    rbs = seq // tq
    w = NA_STEP_HEADS * NA_HEAD_DIM
    ctx_blk0 = (n_batch * seq) // ctx_len
    case = lambda rb: jnp.where(rb == 0, 0, jnp.where(rb == rbs - 1, 2, 1))
    return pl.pallas_call(
        functools.partial(_na_kernel, rows=rows),
        grid=(n_batch, NA_HEADS // NA_STEP_HEADS, rbs),
        in_specs=[pl.BlockSpec((tq, w), lambda b, h, r: (b * rbs + r, COL_NA_Q // w + h)),
                  pl.BlockSpec((seq, w), lambda b, h, r: (b, COL_NA_K // w + h)),
                  pl.BlockSpec((seq, w), lambda b, h, r: (b, COL_NA_V // w + h)),
                  pl.BlockSpec((ctx_len, w), lambda b, h, r: (ctx_blk0 + b, COL_NA_K // w + h)),
                  pl.BlockSpec((ctx_len, w), lambda b, h, r: (ctx_blk0 + b, COL_NA_V // w + h)),
                  pl.BlockSpec((NA_STEP_HEADS, 1, tq, NA_KROWS * GRID_W), lambda b, h, r: (h, case(r), 0, 0))],
        out_specs=pl.BlockSpec((tq, w), lambda b, h, r: (b * rbs + r, h)),
        out_shape=jax.ShapeDtypeStruct((n_batch * seq, NA_WIDTH), BF),
        compiler_params=_params(("parallel", "parallel", "arbitrary")),
        name="na",
    )(p, p, p, p, p, bias)


def _cattn_kernel(q_ref, k_ref, v_ref, o_ref):
    q = (q_ref[...].astype(F32) * (NA_HEAD_DIM ** -0.5)).astype(BF)
    s = _dot_nt(q, k_ref[...])
    p = jnp.exp(s - jnp.max(s, axis=-1, keepdims=True))
    l = jnp.sum(p, axis=-1, keepdims=True)
    o_ref[...] = (_dot(p.astype(BF), v_ref[...]) / l).astype(o_ref.dtype)


def _cattn(p, *, n_batch, seq, ctx_len):
    hd = NA_HEAD_DIM
    blk0 = (n_batch * seq) // ctx_len
    return pl.pallas_call(
        _cattn_kernel,
        grid=(n_batch, NA_HEADS),
        in_specs=[pl.BlockSpec((ctx_len, hd), lambda b, h: (blk0 + b, COL_NA_Q // hd + h)),
                  pl.BlockSpec((ctx_len, hd), lambda b, h: (blk0 + b, COL_NA_K // hd + h)),
                  pl.BlockSpec((ctx_len, hd), lambda b, h: (blk0 + b, COL_NA_V // hd + h))],
        out_specs=pl.BlockSpec((ctx_len, hd), lambda b, h: (b, h)),
        out_shape=jax.ShapeDtypeStruct((n_batch * ctx_len, NA_WIDTH), BF),
        compiler_params=_params(("parallel", "parallel")),
        name="cattn",
    )(p, p, p)


def _bcast_rows(ref, idx_of_group, group, n_groups, lane0=0):
    return jnp.concatenate(
        [jnp.broadcast_to(ref[idx_of_group(g):idx_of_group(g) + 1, lane0:lane0 + LANES], (group, LANES))
         for g in range(n_groups)], axis=0)


def _masked(x, mask):
    return x if mask is None else x * mask


def _in_block_exact(q, k, cum, v_src, fwd, cum_ref, k_ref):
    cs = q.shape[0]
    nh = len(v_src)
    bs = SCAN_BLOCK
    nb = cs // bs
    n_half = bs // SUBLANES
    assert n_half == 2
    d_idx = lax.broadcasted_iota(jnp.int32, (LANES, LANES * nh), 0)
    j_idx = lax.broadcasted_iota(jnp.int32, (LANES, LANES * nh), 1)
    ones_mat = jnp.where(d_idx // (LANES // nh) == j_idx // LANES, 1.0, 0.0).astype(BF)

    def half(x, h):
        return jnp.concatenate([x[b * bs + h * SUBLANES:b * bs + (h + 1) * SUBLANES] for b in range(nb)],
                               axis=0)

    q_h = [half(q, h) for h in range(n_half)]
    c_h = [half(cum, h) for h in range(n_half)]
    tl = lax.broadcasted_iota(jnp.int32, (nb * SUBLANES, LANES), 0) & (SUBLANES - 1)
    acc = [[jnp.zeros((nb * SUBLANES, LANES), F32) for _ in range(n_half)] for _ in range(nh)]
    for s in range(bs):
        s_half, s_loc = divmod(s, SUBLANES)
        at = lambda b: b * bs + s
        c_s = _bcast_rows(cum_ref, at, SUBLANES, nb)
        k_s = _bcast_rows(k_ref, at, SUBLANES, nb)
        v_s = [_bcast_rows(ref, at, SUBLANES, nb, lane0) for ref, lane0 in v_src]
        halves = [h for h in range(n_half) if (h >= s_half if fwd else h <= s_half)]
        parts = []
        for h in halves:
            x = c_h[h] - c_s
            if h == s_half and s_loc != (0 if fwd else SUBLANES - 1):
                x = jnp.where(tl >= s_loc if fwd else tl <= s_loc, x, NEG)
            parts.append((q_h[h] * k_s * jnp.exp2(x)).astype(BF))
        r = _dot(jnp.concatenate(parts, axis=0), ones_mat)
        for i, h in enumerate(halves):
            rh = r[i * nb * SUBLANES:(i + 1) * nb * SUBLANES]
            for j in range(nh):
                acc[j][h] = acc[j][h] + rh[:, j * LANES:(j + 1) * LANES] * v_s[j]
    return [jnp.concatenate([acc[j][h][b * SUBLANES:(b + 1) * SUBLANES]
                             for b in range(nb) for h in range(n_half)], axis=0) for j in range(nh)]


def _in_block_factored(q, k, x0, vs, masks, fwd, block):
    cs = q.shape[0]
    row = lax.broadcasted_iota(jnp.int32, (cs, cs), 0)
    col = lax.broadcasted_iota(jnp.int32, (cs, cs), 1)
    ordered = row >= col if fwd else row <= col
    qf = q * jnp.exp2(x0)
    kf = (k * jnp.exp2(-x0)).astype(BF)
    outs = []
    for j in range(len(vs)):
        a = jnp.where(ordered, _dot_nt(_masked(qf, masks[j]).astype(BF), kf), 0.0)
        if block < cs:
            a = jnp.where((row ^ col) < block, a, 0.0)
        outs.append(_dot(a.astype(BF), vs[j].astype(BF)))
    return outs


def _levels(q, k, cum, vs, masks, fwd, cum_ref, m_lo, m_hi):
    cs = q.shape[0]
    nh = len(vs)
    bs = SCAN_BLOCK
    slabs = [[None] * (cs // bs) for _ in range(nh)]
    m = m_lo
    while m < m_hi:
        pair = 2 * m
        n_pairs = cs // pair
        late_off, early_off = (m, 0) if fwd else (0, m)
        bidx = m - 1 if fwd else m

        def take(x, off):
            return jnp.concatenate([x[p * pair + off:p * pair + off + m] for p in range(n_pairs)], axis=0)

        cb = _bcast_rows(cum_ref, lambda p: p * pair + bidx, m, n_pairs)
        qt = take(q, late_off) * jnp.exp2(take(cum, late_off) - cb)
        kt = (take(k, early_off) * jnp.exp2(cb - take(cum, early_off))).astype(BF)
        if n_pairs > 1:
            rh = lax.broadcasted_iota(jnp.int32, (cs // 2, cs // 2), 0)
            ch = lax.broadcasted_iota(jnp.int32, (cs // 2, cs // 2), 1)
            same_pair = (rh ^ ch) < m
        for j in range(nh):
            a = _dot_nt(_masked(qt, masks[j]).astype(BF), kt)
            if n_pairs > 1:
                a = jnp.where(same_pair, a, 0.0)
            o = _dot(a.astype(BF), take(vs[j], early_off).astype(BF))
            for p in range(n_pairs):
                for i in range(m // bs):
                    idx = (p * pair + late_off) // bs + i
                    piece = o[p * m + i * bs:p * m + (i + 1) * bs]
                    slabs[j][idx] = piece if slabs[j][idx] is None else slabs[j][idx] + piece
        m = pair
    return slabs


class _Chain:
    def __init__(self, scratch, c, nh):
        st_ref, cum_ref, k_ref, q_ref, rest_ref = scratch
        self.st = st_ref.at[c * nh:(c + 1) * nh]
        self.cum, self.k, self.q = cum_ref.at[c], k_ref.at[c], q_ref.at[c]
        self.rest = rest_ref.at[c * nh:(c + 1) * nh]


def _scan_chunk(q, k, lg, vs, v_src, masks, dirn, ch, write_out, fast_block, mid_block=None):
    cs = q.shape[0]
    nh = len(vs)
    bs = SCAN_BLOCK
    fwd = dirn == 0
    row = lax.broadcasted_iota(jnp.int32, (cs, cs), 0)
    col = lax.broadcasted_iota(jnp.int32, (cs, cs), 1)
    tri = jnp.where(row >= col if fwd else row <= col, 1.0, 0.0).astype(BF)
    hi = lg.astype(BF)
    r1 = lg - hi.astype(F32)
    mid = r1.astype(BF)
    lo = (r1 - mid.astype(F32)).astype(BF)
    cum = (_dot(tri, hi) + _dot(tri, mid) + _dot(tri, lo)) * LOG2E

    ch.cum[...] = cum
    ch.k[...] = k
    ch.q[...] = q

    first = 0 if fwd else fast_block - 1
    x0 = cum - _bcast_rows(ch.cum, lambda b: b * fast_block + first, fast_block, cs // fast_block)
    span = jnp.max(-x0)
    in_block = _in_block_factored(q, k, x0, vs, masks, fwd, fast_block)

    slabs = _levels(q, k, cum, vs, masks, fwd, ch.cum, fast_block, cs)

    last_row = cs - 1 if fwd else 0
    last = ch.cum[last_row:last_row + 1, :]
    qh = q * jnp.exp2(cum)
    kh = k * jnp.exp2(last - cum)
    dec = jnp.exp2(last)
    outs = []
    for j in range(nh):
        st = ch.st[j]
        vj = vs[j].astype(BF)
        rest = _join_slabs(slabs[j]) + _dot_nt(_masked(qh, masks[j]).astype(BF), st.astype(BF))
        ch.st[j] = st * dec + _dot_tn(vj, _masked(kh, masks[j]).astype(BF))
        ch.rest[j] = rest
        outs.append(rest + in_block[j])
    write_out(outs)

    def fixup():
        @pl.when(jnp.logical_not(span <= SAFE_LOG2_SPAN))
        def _():
            q_, k_, cum_ = ch.q[...], ch.k[...], ch.cum[...]
            vs_ = [ref[:, lane0:lane0 + LANES] for ref, lane0 in v_src]

            def rewrite(in_blk, lo):
                low = _levels(q_, k_, cum_, vs_, masks, fwd, ch.cum, lo, fast_block)
                write_out([ch.rest[j] + in_blk[j] + _join_slabs(low[j]) for j in range(nh)])

            def pairwise():
                rewrite(_in_block_exact(q_, k_, cum_, v_src, fwd, ch.cum, ch.k), bs)

            if mid_block is None:
                pairwise()
            else:
                first_m = 0 if fwd else mid_block - 1
                xm = cum_ - _bcast_rows(ch.cum, lambda b: b * mid_block + first_m, mid_block, cs // mid_block)
                span_m = jnp.max(-xm)
                pl.when(span_m <= SAFE_LOG2_SPAN)(
                    lambda: rewrite(_in_block_factored(q_, k_, xm, vs_, masks, fwd, mid_block), mid_block))
                pl.when(jnp.logical_not(span_m <= SAFE_LOG2_SPAN))(pairwise)

    return fixup


def _join_slabs(slabs):
    if all(sl is None for sl in slabs):
        return 0.0
    return jnp.concatenate([jnp.zeros((SCAN_BLOCK, LANES), F32) if sl is None else sl for sl in slabs],
                           axis=0)


def _hg_kernel(qf_ref, vf_ref, zf_ref, qb_ref, vb_ref, zb_ref, lbp_ref, of_ref, ob_ref, *scratch, layer):
    @pl.when(pl.program_id(2) == 0)
    def _():
        scratch[0][...] = jnp.zeros_like(scratch[0])

    fixups = []
    for dirn, (q_ref, v_ref, z_ref, o_ref) in enumerate(((qf_ref, vf_ref, zf_ref, of_ref),
                                                         (qb_ref, vb_ref, zb_ref, ob_ref))):
        p = lbp_ref[dirn]
        e = jnp.exp(p - jnp.max(p, axis=0, keepdims=True))
        den = jnp.sum(e, axis=0, keepdims=True)
        lb_all = jnp.zeros_like(den)
        for i in range(1, layer + 1):
            lb_all = lb_all + e[i:i + 1]
        lb_all = lb_all / den
        for hh in range(HG_STEP_HEADS):
            lanes = slice(hh * LANES, (hh + 1) * LANES)
            lb = lb_all[:, lanes]
            z = z_ref[:, lanes]
            en = jnp.exp(-jnp.abs(z))
            big = 1.0 / (1.0 + en)
            small = en * big
            sig = jnp.where(z >= 0, big, small)
            nsig = jnp.where(z >= 0, small, big)
            k = (1.0 - lb) * nsig
            lg = jnp.log(lb + (1.0 - lb) * sig)
            qr = q_ref[:, lanes]
            q = qr * _sigmoid(qr)

            def write_out(outs, o_ref=o_ref, lanes=lanes):
                o_ref[:, lanes] = outs[0]

            ch = _Chain(scratch, dirn * HG_STEP_HEADS + hh, 1)
            fixups.append(_scan_chunk(q, k, lg, [v_ref[:, lanes]], [(v_ref, hh * LANES)], [None], dirn,
                                      ch, write_out, HG_FAST_BLOCK, HG_MID_BLOCK))
    for fixup in fixups:
        fixup()


def _gla_kernel(qf_ref, kf_ref, vf_ref, rf_ref, cosf_ref, sinf_ref,
                qb_ref, kb_ref, vb_ref, rb_ref, cosb_ref, sinb_ref, up_ref, gb_ref, of_ref, ob_ref, *scratch):
    @pl.when(pl.program_id(2) == 0)
    def _():
        scratch[0][...] = jnp.zeros_like(scratch[0])

    cs = qf_ref.shape[0]
    lane = lax.broadcasted_iota(jnp.int32, (cs, LANES), 1)
    first = (lane & 16) == 0
    lane1 = lax.broadcasted_iota(jnp.int32, (1, LANES), 1)
    masks = [jnp.where(lane1 < GLA_HEAD_K, 1.0, 0.0), jnp.where(lane1 >= GLA_HEAD_K, 1.0, 0.0)]
    fixups = []
    for dirn, (q_ref, k_ref, v_ref, r_ref, cos_ref, sin_ref, o_ref) in enumerate(
            ((qf_ref, kf_ref, vf_ref, rf_ref, cosf_ref, sinf_ref, of_ref),
             (qb_ref, kb_ref, vb_ref, rb_ref, cosb_ref, sinb_ref, ob_ref))):
        cos = cos_ref[...]
        sin = sin_ref[...]

        def rope(x):
            partner = jnp.where(first, pltpu.roll(x, LANES - 16, 1), pltpu.roll(x, 16, 1))
            return x * cos + partner * sin

        q = rope(q_ref[...] * (GLA_HEAD_K ** -0.5))
        k = rope(k_ref[...])
        pre = _dot(r_ref[...].astype(BF), up_ref[dirn].astype(BF)) + gb_ref[dirn]
        lg = (jnp.minimum(pre, 0.0) - jnp.log(1.0 + jnp.exp(-jnp.abs(pre)))) * (1.0 / GLA_GATE_NORM)

        def write_out(outs, o_ref=o_ref):
            o_ref[...] = jnp.concatenate(outs, axis=1)

        ch = _Chain(scratch, dirn, 2)
        fixups.append(_scan_chunk(q, k, lg, [v_ref[:, :LANES], v_ref[:, LANES:]],
                                  [(v_ref, 0), (v_ref, LANES)], masks, dirn, ch, write_out, GLA_FAST_BLOCK))
    for fixup in fixups:
        fixup()


HG_STEP_HEADS = 2


def _scan_scratch(n_chains, nh):
    cs = SCAN_CHUNK
    return [pltpu.VMEM((n_chains * nh, LANES, LANES), F32), pltpu.VMEM((n_chains, cs, LANES), F32),
            pltpu.VMEM((n_chains, cs, LANES), F32), pltpu.VMEM((n_chains, cs, LANES), F32),
            pltpu.VMEM((n_chains * nh, cs, LANES), F32)]


def _scan_rowblks(nc, ctx_blk0):
    fwd = lambda b, c: jnp.where(c == 0, ctx_blk0 + b, b * nc + c - 1)
    bwd = lambda b, c: jnp.where(c == 0, ctx_blk0 + b, b * nc + nc - c)
    return fwd, bwd


def _hg_scan(p, lbp, *, layer, n_batch, seq, n_all):
    cs = SCAN_CHUNK
    nc = seq // cs
    w = HG_STEP_HEADS * LANES
    rbs = _scan_rowblks(nc, (n_batch * seq) // cs)
    depth = lbp.shape[1]
    col = lambda c0: c0 // w

    def specs(rb, zcol):
        return [pl.BlockSpec((cs, w), lambda b, h, c: (rb(b, c), col(COL_HG_Q) + h)),
                pl.BlockSpec((cs, w), lambda b, h, c: (rb(b, c), col(COL_HG_I) + h)),
                pl.BlockSpec((cs, w), lambda b, h, c: (rb(b, c), col(zcol) + h))]

    return pl.pallas_call(
        functools.partial(_hg_kernel, layer=layer),
        grid=(n_batch, HG_HEADS // HG_STEP_HEADS, nc + 1),
        in_specs=specs(rbs[0], COL_HG_ZF) + specs(rbs[1], COL_HG_ZF + HG_WIDTH)
        + [pl.BlockSpec((2, depth, w), lambda b, h, c: (0, 0, h))],
        out_specs=[pl.BlockSpec((cs, w), lambda b, h, c: (rbs[0](b, c), h)),
                   pl.BlockSpec((cs, w), lambda b, h, c: (rbs[1](b, c), h))],
        out_shape=[jax.ShapeDtypeStruct((n_all, HG_WIDTH), F32)] * 2,
        scratch_shapes=_scan_scratch(2 * HG_STEP_HEADS, 1),
        compiler_params=_params(("parallel", "parallel", "arbitrary")),
        name="hg_scan",
    )(p, p, p, p, p, p, lbp)


def _gla_scan(p, r, up_pad, gate_b, cos_t, sin_t, *, n_batch, seq, n_all):
    cs = SCAN_CHUNK
    nc = seq // cs
    rbs = _scan_rowblks(nc, (n_batch * seq) // cs)
    tbs = (lambda b, c: jnp.where(c == 0, nc, c - 1), lambda b, c: jnp.where(c == 0, nc, nc - c))

    def specs(rb, tb):
        return [pl.BlockSpec((cs, LANES), lambda b, h, c: (rb(b, c), COL_GL_Q // LANES + h)),
                pl.BlockSpec((cs, LANES), lambda b, h, c: (rb(b, c), COL_GL_K // LANES + h)),
                pl.BlockSpec((cs, 2 * LANES), lambda b, h, c: (rb(b, c), COL_GL_V // (2 * LANES) + h)),
                pl.BlockSpec((cs, LANES), lambda b, h, c: (rb(b, c), 0)),
                pl.BlockSpec((cs, LANES), lambda b, h, c: (tb(b, c), 0)),
                pl.BlockSpec((cs, LANES), lambda b, h, c: (tb(b, c), 0))]

    return pl.pallas_call(
        _gla_kernel,
        grid=(n_batch, GLA_HEADS // 2, nc + 1),
        in_specs=specs(rbs[0], tbs[0]) + specs(rbs[1], tbs[1])
        + [pl.BlockSpec((2, LANES, LANES), lambda b, h, c: (0, 0, h)),
           pl.BlockSpec((2, 1, LANES), lambda b, h, c: (0, 0, h))],
        out_specs=[pl.BlockSpec((cs, 2 * LANES), lambda b, h, c: (rbs[0](b, c), h)),
                   pl.BlockSpec((cs, 2 * LANES), lambda b, h, c: (rbs[1](b, c), h))],
        out_shape=[jax.ShapeDtypeStruct((n_all, GLA_WIDTH), F32)] * 2,
        scratch_shapes=_scan_scratch(2, 2),
        compiler_params=_params(("parallel", "parallel", "arbitrary")),
        name="gla_scan",
    )(p, p, p, r, cos_t, sin_t, p, p, p, r, cos_t, sin_t, up_pad, gate_b)


def _rope_tables(seq, ctx_len):
    half = GLA_HEAD_K // 2
    inv = ROPE_BASE ** (-jnp.arange(0, half, 2, dtype=F32) / half)
    pos = jnp.arange(seq)
    ang_r = (pos // GRID_W).astype(F32)[:, None] * inv
    ang_c = (pos % GRID_W).astype(F32)[:, None] * inv
    cos = jnp.concatenate([jnp.cos(ang_r)] * 2 + [jnp.cos(ang_c)] * 2, axis=1)
    sin = jnp.concatenate([-jnp.sin(ang_r), jnp.sin(ang_r), -jnp.sin(ang_c), jnp.sin(ang_c)], axis=1)
    cos = jnp.concatenate([jnp.tile(cos, (1, 2)), jnp.ones((ctx_len, LANES), F32)], axis=0)
    sin = jnp.concatenate([jnp.tile(sin, (1, 2)), jnp.zeros((ctx_len, LANES), F32)], axis=0)
    return cos, sin


def _outproj_kernel(na_ref, nc_ref, hgf_ref, hgb_ref, ghg_ref, glf_ref, glb_ref, ggl_ref, nhg_ref, ngl_ref,
                    w_ref, x_ref, gt_ref, sh_ref, sc_ref, n2_ref, x1_ref, h2_ref, edge_ref, mix_ref, *,
                    n_lat_tiles):
    @pl.when(pl.program_id(0) < n_lat_tiles)
    def _():
        mix_ref[:, :NA_WIDTH] = na_ref[...]

    @pl.when(pl.program_id(0) >= n_lat_tiles)
    def _():
        mix_ref[:, :NA_WIDTH] = nc_ref[...]

    def norm_gate(of_ref, ob_ref, g_ref, gain_ref, col0, n_heads):
        o = of_ref[...] + ob_ref[...]
        g = g_ref[...]
        for h in range(n_heads):
            sl = slice(h * LANES, (h + 1) * LANES)
            gh = g[:, sl]
            y = _rms(o[:, sl]) * gain_ref[...]
            mix_ref[:, col0 + h * LANES:col0 + (h + 1) * LANES] = (y * (gh * _sigmoid(gh))).astype(BF)

    norm_gate(hgf_ref, hgb_ref, ghg_ref, nhg_ref, NA_WIDTH, HG_HEADS)
    norm_gate(glf_ref, glb_ref, ggl_ref, ngl_ref, NA_WIDTH + HG_WIDTH, GLA_HEADS)
    x1 = x_ref[...] + gt_ref[0] * _dot(mix_ref[...], w_ref[0])
    x1_ref[...] = x1
    y = _rms(x1) * n2_ref[...]
    h2 = y * (1.0 + sc_ref[0]) + sh_ref[0]
    h2_ref[...] = h2.astype(BF)
    edge_ref[0] = jnp.concatenate([h2[:1], h2[-1:]], axis=0)


OUT_TM = 256


def _outproj(na_o, na_ctx, o_hg, o_gl, p, hg_gain, gl_gain, w_out, x_all, mod, norm2_g, *,
             layer, n_rows, seq, n_batch):
    d = D_MODEL
    tm = OUT_TM
    tpb = seq // tm
    n_lat_tiles = n_batch * tpb
    mod_row = lambda k: (lambda i: (jnp.minimum(i // tpb, n_batch) * 6 + k, 0, 0))
    return pl.pallas_call(
        functools.partial(_outproj_kernel, n_lat_tiles=n_lat_tiles),
        grid=(n_rows // tm,),
        in_specs=[pl.BlockSpec((tm, NA_WIDTH), lambda i: (jnp.minimum(i, n_lat_tiles - 1), 0)),
                  pl.BlockSpec((tm, NA_WIDTH), lambda i: (jnp.maximum(i - n_lat_tiles, 0), 0)),
                  pl.BlockSpec((tm, HG_WIDTH), lambda i: (i, 0)),
                  pl.BlockSpec((tm, HG_WIDTH), lambda i: (i, 0)),
                  pl.BlockSpec((tm, HG_WIDTH), lambda i: (i, COL_HG_G // HG_WIDTH)),
                  pl.BlockSpec((tm, GLA_WIDTH), lambda i: (i, 0)),
                  pl.BlockSpec((tm, GLA_WIDTH), lambda i: (i, 0)),
                  pl.BlockSpec((tm, GLA_WIDTH), lambda i: (i, COL_GL_G // GLA_WIDTH)),
                  pl.BlockSpec((1, LANES), lambda i: (0, 0)),
                  pl.BlockSpec((1, LANES), lambda i: (0, 0)),
                  pl.BlockSpec((1, d, d), lambda i: (layer, 0, 0)),
                  pl.BlockSpec((tm, d), lambda i: (i, 0)),
                  pl.BlockSpec((1, 1, d), mod_row(2)),
                  pl.BlockSpec((1, 1, d), mod_row(3)),
                  pl.BlockSpec((1, 1, d), mod_row(4)),
                  pl.BlockSpec((1, d), lambda i: (0, 0))],
        out_specs=[pl.BlockSpec((tm, d), lambda i: (i, 0)),
                   pl.BlockSpec((tm, d), lambda i: (i, 0)),
                   pl.BlockSpec((1, 2, d), lambda i: (i, 0, 0))],
        out_shape=[jax.ShapeDtypeStruct((n_rows, d), F32),
                   jax.ShapeDtypeStruct((n_rows, d), BF),
                   jax.ShapeDtypeStruct((n_rows // tm, 2, d), F32)],
        scratch_shapes=[pltpu.VMEM((tm, d), BF)],
        compiler_params=_params(("parallel",)),
        name="outproj",
    )(na_o, na_ctx, o_hg[0], o_hg[1], p, o_gl[0], o_gl[1], p, hg_gain, gl_gain, w_out, x_all, mod, mod, mod,
      norm2_g)


def _patch_row(x, r, row_in_slab, value, keep=None):
    r0 = (r // SUBLANES) * SUBLANES
    slab = x[r0:r0 + SUBLANES]
    new = value if keep is None else slab * keep + value * (1.0 - keep)
    slab = jnp.where(row_in_slab == r - r0, new, slab)
    parts = [x[:r0]] * (r0 > 0) + [slab] + [x[r0 + SUBLANES:]] * (r0 + SUBLANES < x.shape[0])
    return jnp.concatenate(parts, axis=0)


def _ffn_kernel(h_ref, halo_ref, wa_ref, wb_ref, cwa_ref, cwb_ref, cba_ref, cbb_ref, wd_ref,
                x1_ref, gt_ref, fg_ref, o_ref, acc_ref, hc_ref, *, n_lat_tiles, ctx_len, final_norm):
    i = pl.program_id(0)
    j = pl.program_id(1)
    tm = h_ref.shape[0]
    tf = wa_ref.shape[-1]

    @pl.when(j == 0)
    def _():
        acc_ref[...] = jnp.zeros_like(acc_ref)
        hc_ref[:tm] = h_ref[...]
        hc_ref[tm:] = halo_ref[0]

    hc = hc_ref[...]
    row8 = lax.broadcasted_iota(jnp.int32, (SUBLANES, tf), 0)
    keep_lat = jnp.where(i < n_lat_tiles, 1.0, 0.0)
    zero_row = jnp.zeros((1, tf), F32)

    def conv(w_ref, cw_ref, cb_ref):
        u_all = _dot(hc, w_ref[0])
        u = u_all[:tm]
        prev = _patch_row(pltpu.roll(u, 1, 0), 0, row8, u_all[tm:tm + 1])
        nxt = _patch_row(pltpu.roll(u, tm - 1, 0), tm - 1, row8, u_all[tm + 1:tm + 2])
        for r in range(ctx_len, tm, ctx_len):
            prev = _patch_row(prev, r, row8, zero_row, keep_lat)
            nxt = _patch_row(nxt, r - 1, row8, zero_row, keep_lat)
        cw = cw_ref[...]
        return cw[0:1] * prev + cw[1:2] * u + cw[2:3] * nxt + cb_ref[...]

    a = conv(wa_ref, cwa_ref, cba_ref)
    b = conv(wb_ref, cwb_ref, cbb_ref)
    g = (a * _sigmoid(a) * b).astype(BF)
    acc_ref[...] += _dot(g, wd_ref[0])

    @pl.when(j == pl.num_programs(1) - 1)
    def _():
        x2 = x1_ref[...] + gt_ref[0] * acc_ref[...]
        if final_norm:
            x2 = _rms(x2) * fg_ref[...]
        o_ref[...] = x2


FFN_TM = 512
FFN_HALO = 2 * SUBLANES


def _ffn(h2, halo, w_up, conv_w, conv_b, w_down, x1, mod, final_g, *, layer, n_rows, n_lat_tiles,
         seq, n_batch, ctx_len, final_norm):
    d = D_MODEL
    tm, tf = FFN_TM, 512
    nf = D_FF // tf
    tpb = seq // tm
    mod_row = lambda i, j: (jnp.minimum(i // tpb, n_batch) * 6 + 5, 0, 0)
    return pl.pallas_call(
        functools.partial(_ffn_kernel, n_lat_tiles=n_lat_tiles, ctx_len=ctx_len,
                          final_norm=final_norm),
        grid=(n_rows // tm, nf),
        in_specs=[pl.BlockSpec((tm, d), lambda i, j: (i, 0)),
                  pl.BlockSpec((1, FFN_HALO, d), lambda i, j: (i, 0, 0)),
                  pl.BlockSpec((1, d, tf), lambda i, j: (layer, 0, j)),
                  pl.BlockSpec((1, d, tf), lambda i, j: (layer, 0, nf + j)),
                  pl.BlockSpec((3, tf), lambda i, j: (0, j)),
                  pl.BlockSpec((3, tf), lambda i, j: (0, nf + j)),
                  pl.BlockSpec((1, tf), lambda i, j: (0, j)),
                  pl.BlockSpec((1, tf), lambda i, j: (0, nf + j)),
                  pl.BlockSpec((1, tf, d), lambda i, j: (layer, j, 0)),
                  pl.BlockSpec((tm, d), lambda i, j: (i, 0)),
                  pl.BlockSpec((1, 1, d), mod_row),
                  pl.BlockSpec((1, d), lambda i, j: (0, 0))],
        out_specs=pl.BlockSpec((tm, d), lambda i, j: (i, 0)),
        out_shape=jax.ShapeDtypeStruct((n_rows, d), F32),
        scratch_shapes=[pltpu.VMEM((tm, d), F32), pltpu.VMEM((tm + FFN_HALO, d), BF)],
        compiler_params=_params(("parallel", "arbitrary")),
        name="ffn",
    )(h2, halo, w_up, w_up, conv_w, conv_w, conv_b, conv_b, w_down, x1, mod, final_g)


def _ffn_halo(edges, n_tiles, n_lat, seq, ctx_len):
    k = FFN_TM // OUT_TM
    d = edges.shape[-1]
    starts = np.arange(n_tiles) * FFN_TM
    seq_of = np.where(starts < n_lat, seq, ctx_len)
    rel = np.where(starts < n_lat, starts, starts - n_lat)
    prev_ok = jnp.asarray((rel % seq_of != 0).astype(np.float32))[:, None]
    next_ok = jnp.asarray(((rel + FFN_TM) % seq_of != 0).astype(np.float32))[:, None]
    zero = jnp.zeros((1, d), edges.dtype)
    last_rows = edges[k - 1::k, 1][:n_tiles]
    first_rows = edges[0::k, 0][:n_tiles]
    prev = jnp.concatenate([zero, last_rows[:-1]], axis=0) * prev_ok
    nxt = jnp.concatenate([first_rows[1:], zero], axis=0) * next_ok
    pad = jnp.zeros((n_tiles, FFN_HALO - 2, d), edges.dtype)
    return jnp.concatenate([prev[:, None], nxt[:, None], pad], axis=1).astype(BF)


def kernel(x, c, ctx, c_ctx, ada_w, ada_b, norm1_g, w_in, na_rpb, hg_lower_bounds, hg_norm_g,
           gla_gate_up, gla_gate_b, gla_norm_g, w_out, norm2_g, w_up, conv_w, conv_b, w_down, final_g):
    n_batch, seq, d = x.shape
    ctx_len = ctx.shape[1]
    depth = ada_w.shape[0]
    n_lat = n_batch * seq
    n_all = n_lat + n_batch * ctx_len
    assert d == D_MODEL and ctx_len == SCAN_CHUNK and seq % (NA_QROWS * GRID_W) == 0
    assert seq // GRID_W >= NA_KROWS + NA_QROWS and n_batch < SUBLANES

    x_all = jnp.concatenate([x.reshape(n_lat, d), ctx.reshape(n_batch * ctx_len, d)], axis=0)
    c_all = jnp.concatenate([c, c_ctx[None], jnp.zeros((SUBLANES - n_batch - 1, d), F32)], axis=0)
    mods = _ada(c_all, ada_w, ada_b)
    cos_t, sin_t = _rope_tables(seq, ctx_len)
    w_in_b, w_out_b, w_up_b, w_down_b = (w.astype(BF) for w in (w_in, w_out, w_up, w_down))

    for l in range(depth):
        last = l == depth - 1
        mod = mods[l].reshape(SUBLANES * 6, 1, d)
        w_rank = jnp.pad(w_in[l, :, MAIN_COLS:], ((0, 0), (0, LANES - 2 * GLA_RANK))).astype(BF)
        p_na, p, r = _inproj(x_all, norm1_g[l][None], mod, w_in_b, w_rank, layer=l, seq=seq, n_batch=n_batch)

        bias = _na_bias_table(na_rpb[l], seq // GRID_W)
        na_o = _na(p_na, bias, n_batch=n_batch, seq=seq, ctx_len=ctx_len)
        na_ctx = na_o if last else _cattn(p_na, n_batch=n_batch, seq=seq, ctx_len=ctx_len)

        o_hg = _hg_scan(p, hg_lower_bounds, layer=l, n_batch=n_batch, seq=seq, n_all=n_all)
        up = gla_gate_up[l]
        up_pad = jnp.zeros((2, LANES, GLA_KEY_WIDTH), F32)
        up_pad = up_pad.at[0, :GLA_RANK].set(up[0]).at[1, GLA_RANK:2 * GLA_RANK].set(up[1])
        o_gl = _gla_scan(p, r, up_pad, gla_gate_b[l][:, None, :], cos_t, sin_t,
                         n_batch=n_batch, seq=seq, n_all=n_all)

        n_rows = n_lat if last else n_all
        x1, h2, edges = _outproj(na_o, na_ctx, o_hg, o_gl, p, hg_norm_g[l][None], gla_norm_g[l][None],
                                 w_out_b, x_all, mod, norm2_g[l][None], layer=l, n_rows=n_rows,
                                 seq=seq, n_batch=n_batch)
        halo = _ffn_halo(edges, n_rows // FFN_TM, n_lat, seq, ctx_len)
        x_all = _ffn(h2, halo, w_up_b, conv_w[l], conv_b[l][None], w_down_b,
                     x1, mod, final_g[None], layer=l, n_rows=n_rows, n_lat_tiles=n_lat // FFN_TM,
                     seq=seq, n_batch=n_batch, ctx_len=ctx_len, final_norm=last)
    return x_all[:n_lat].reshape(n_batch, seq, d)
```

```python
import functools

import numpy as np
import jax
import jax.numpy as jnp
from jax import lax
from jax.experimental import pallas as pl
from jax.experimental.pallas import tpu as pltpu

F32 = jnp.float32
BF = jnp.bfloat16

D_MODEL = 2048
GRID_W = 64
NA_HEADS = 8
NA_HEAD_DIM = 128
NA_WIDTH = NA_HEADS * NA_HEAD_DIM
NA_ROWS = 8
NA_COLS = 16
HG_WIDTH = 512
HG_HEADS = 4
GLA_WIDTH = 512
GLA_HEADS = 4
GLA_KEY_WIDTH = 256
GLA_HEAD_K = 64
GLA_RANK = 16
GLA_GATE_NORM = 16.0
D_FF = 5632
ROPE_BASE = 10000.0
EPS = 1e-6
MAIN_COLS = 7168
NA_COLS_ALL = 3 * NA_WIDTH
COL_NA_Q, COL_NA_K, COL_NA_V = 0, 1024, 2048
COL_HG_Q, COL_HG_I, COL_HG_ZF, COL_HG_G = 0, 512, 1024, 2048
COL_GL_Q, COL_GL_K, COL_GL_V, COL_GL_G = 2560, 2816, 3072, 3584

LANES = 128
SUBLANES = 8
VMEM_LIMIT = 56 * 1024 * 1024

NA_QROWS = 8
NA_KROWS = 16
SCAN_CHUNK = 256
SCAN_BLOCK = 16
NEG = -1e30
LOG2E = 1.4426950408889634
SAFE_LOG2_SPAN = 96.0
HG_FAST_BLOCK = 32
HG_MID_BLOCK = 16
GLA_FAST_BLOCK = SCAN_CHUNK


def _dot(a, b):
    return jnp.dot(a, b, preferred_element_type=F32)


def _dot_nt(a, b):
    return lax.dot_general(a, b, (((1,), (1,)), ((), ())), preferred_element_type=F32)


def _dot_tn(a, b):
    return lax.dot_general(a, b, (((0,), (0,)), ((), ())), preferred_element_type=F32)


def _sigmoid(x):
    return 1.0 / (1.0 + jnp.exp(-x))


def _rms(x):
    return x * lax.rsqrt(jnp.mean(x * x, axis=-1, keepdims=True) + EPS)


def _params(sem):
    return pltpu.CompilerParams(dimension_semantics=sem, vmem_limit_bytes=VMEM_LIMIT)


def _ada_kernel(c_ref, w_ref, b_ref, o_ref):
    c = c_ref[...]
    s = (c * _sigmoid(c)).astype(BF)
    o_ref[0] = _dot(s, w_ref[0].astype(BF)) + b_ref[0]


def _ada(c_all, ada_w, ada_b):
    depth, d, n = ada_w.shape
    tn = 1024
    return pl.pallas_call(
        _ada_kernel,
        grid=(depth, n // tn),
        in_specs=[pl.BlockSpec((SUBLANES, d), lambda l, j: (0, 0)),
                  pl.BlockSpec((1, d, tn), lambda l, j: (l, 0, j)),
                  pl.BlockSpec((1, 1, tn), lambda l, j: (l, 0, j))],
        out_specs=pl.BlockSpec((1, SUBLANES, tn), lambda l, j: (l, 0, j)),
        out_shape=jax.ShapeDtypeStruct((depth, SUBLANES, n), F32),
        compiler_params=_params(("parallel", "parallel")),
        name="ada",
    )(c_all, ada_w, ada_b.reshape(depth, 1, n))


def _inproj_kernel(x_ref, g_ref, sh_ref, sc_ref, w_ref, wr_ref, pna_ref, prest_ref, r_ref, h_ref,
                   *, n_na_tiles):
    j = pl.program_id(1)

    @pl.when(j == 0)
    def _():
        y = _rms(x_ref[...]) * g_ref[...]
        h = (y * (1.0 + sc_ref[0]) + sh_ref[0]).astype(BF)
        h_ref[...] = h
        r_ref[...] = _dot(h, wr_ref[...])

    acc = _dot(h_ref[...], w_ref[0])

    @pl.when(j < n_na_tiles)
    def _():
        pna_ref[...] = acc.astype(BF)

    @pl.when(j >= n_na_tiles)
    def _():
        prest_ref[...] = acc


def _inproj(x_all, norm_g, mod, w_in, w_rank, *, layer, seq, n_batch):
    n, d = x_all.shape
    tm, tn = 1024, 1024
    n_na = NA_COLS_ALL // tn
    tpb = seq // tm
    mod_row = lambda k: (lambda i, j: (jnp.minimum(i // tpb, n_batch) * 6 + k, 0, 0))
    return pl.pallas_call(
        functools.partial(_inproj_kernel, n_na_tiles=n_na),
        grid=(n // tm, MAIN_COLS // tn),
        in_specs=[pl.BlockSpec((tm, d), lambda i, j: (i, 0)),
                  pl.BlockSpec((1, d), lambda i, j: (0, 0)),
                  pl.BlockSpec((1, 1, d), mod_row(0)),
                  pl.BlockSpec((1, 1, d), mod_row(1)),
                  pl.BlockSpec((1, d, tn), lambda i, j: (layer, 0, j)),
                  pl.BlockSpec((d, LANES), lambda i, j: (0, 0))],
        out_specs=[pl.BlockSpec((tm, tn), lambda i, j: (i, jnp.minimum(j, n_na - 1))),
                   pl.BlockSpec((tm, tn), lambda i, j: (i, jnp.maximum(j - n_na, 0))),
                   pl.BlockSpec((tm, LANES), lambda i, j: (i, 0))],
        out_shape=[jax.ShapeDtypeStruct((n, NA_COLS_ALL), BF),
                   jax.ShapeDtypeStruct((n, MAIN_COLS - NA_COLS_ALL), F32),
                   jax.ShapeDtypeStruct((n, LANES), F32)],
        scratch_shapes=[pltpu.VMEM((tm, d), BF)],
        compiler_params=_params(("parallel", "arbitrary")),
        name="inproj",
    )(x_all, norm_g, mod, mod, w_in, w_rank)


def _na_bias_table(rpb, rows):
    n_heads = rpb.shape[0]
    n_dr = 2 * NA_ROWS - 1
    qc = np.arange(GRID_W)[:, None]
    kc = np.arange(GRID_W)[None, :]
    cstart = np.clip(qc - NA_COLS // 2, 0, GRID_W - NA_COLS)
    col_ok = (kc >= cstart) & (kc < cstart + NA_COLS)
    dc_idx = np.clip(kc - qc, -(NA_COLS - 1), NA_COLS - 1) + NA_COLS - 1
    onehot = (dc_idx[None] == np.arange(2 * NA_COLS - 1)[:, None, None]).astype(np.float32)
    by_col = jnp.einsum('hrj,jqk->hqrk', rpb, jnp.asarray(onehot), precision=lax.Precision.HIGHEST)
    by_col = jnp.where(jnp.asarray(col_ok)[None, :, None, :], by_col, NEG)
    pad = NA_KROWS
    flat = jnp.pad(by_col, ((0, 0), (0, 0), (pad, pad), (0, 0))).reshape(n_heads, GRID_W, -1)
    kr = np.arange(NA_KROWS)
    cases = []
    for r0 in (0, NA_QROWS, rows - NA_QROWS):
        ks = int(np.clip(r0 - NA_ROWS // 2, 0, rows - NA_KROWS))
        per_row = []
        for qr in range(NA_QROWS):
            r = r0 + qr
            rs = int(np.clip(r - NA_ROWS // 2, 0, rows - NA_ROWS))
            row_ok = np.repeat((ks + kr >= rs) & (ks + kr < rs + NA_ROWS), GRID_W)
            lo = ks - r + NA_ROWS - 1 + pad
            assert 0 <= lo and lo + NA_KROWS <= n_dr + 2 * pad
            sl = flat[:, :, lo * GRID_W:(lo + NA_KROWS) * GRID_W]
            per_row.append(jnp.where(jnp.asarray(row_ok)[None, None, :], sl, NEG))
        cases.append(jnp.stack(per_row, axis=1).reshape(n_heads, NA_QROWS * GRID_W, NA_KROWS * GRID_W))
    return jnp.stack(cases, axis=1)


def _na_kernel(q_ref, k_ref, v_ref, kc_ref, vc_ref, bias_ref, o_ref, *, rows):
    rb = pl.program_id(2)
    nk = NA_KROWS * GRID_W
    ks = jnp.clip(rb * NA_QROWS - NA_ROWS // 2, 0, rows - NA_KROWS) * GRID_W
    ks = pl.multiple_of(ks, GRID_W)
    for hh in range(NA_STEP_HEADS):
        lanes = slice(hh * NA_HEAD_DIM, (hh + 1) * NA_HEAD_DIM)
        kblk = k_ref[pl.ds(ks, nk), lanes]
        vblk = v_ref[pl.ds(ks, nk), lanes]
        q = (q_ref[:, lanes].astype(F32) * (NA_HEAD_DIM ** -0.5)).astype(BF)
        s_loc = _dot_nt(q, kblk) + bias_ref[hh, 0]
        s_ctx = _dot_nt(q, kc_ref[:, lanes])
        m = jnp.maximum(jnp.max(s_loc, axis=-1, keepdims=True), jnp.max(s_ctx, axis=-1, keepdims=True))
        p_loc = jnp.exp(s_loc - m)
        p_ctx = jnp.exp(s_ctx - m)
        l = jnp.sum(p_loc, axis=-1, keepdims=True) + jnp.sum(p_ctx, axis=-1, keepdims=True)
        o = _dot(p_loc.astype(BF), vblk) + _dot(p_ctx.astype(BF), vc_ref[:, lanes])
        o_ref[:, lanes] = (o / l).astype(o_ref.dtype)


NA_STEP_HEADS = 2


def _na(p, bias, *, n_batch, seq, ctx_len):
    rows = seq // GRID_W
    tq = NA_QROWS * GRID_W
    rbs = seq // tq
    w = NA_STEP_HEADS * NA_HEAD_DIM
    ctx_blk0 = (n_batch * seq) // ctx_len
    case = lambda rb: jnp.where(rb == 0, 0, jnp.where(rb == rbs - 1, 2, 1))
    return pl.pallas_call(
        functools.partial(_na_kernel, rows=rows),
        grid=(n_batch, NA_HEADS // NA_STEP_HEADS, rbs),
        in_specs=[pl.BlockSpec((tq, w), lambda b, h, r: (b * rbs + r, COL_NA_Q // w + h)),
                  pl.BlockSpec((seq, w), lambda b, h, r: (b, COL_NA_K // w + h)),
                  pl.BlockSpec((seq, w), lambda b, h, r: (b, COL_NA_V // w + h)),
                  pl.BlockSpec((ctx_len, w), lambda b, h, r: (ctx_blk0 + b, COL_NA_K // w + h)),
                  pl.BlockSpec((ctx_len, w), lambda b, h, r: (ctx_blk0 + b, COL_NA_V // w + h)),
                  pl.BlockSpec((NA_STEP_HEADS, 1, tq, NA_KROWS * GRID_W), lambda b, h, r: (h, case(r), 0, 0))],
        out_specs=pl.BlockSpec((tq, w), lambda b, h, r: (b * rbs + r, h)),
        out_shape=jax.ShapeDtypeStruct((n_batch * seq, NA_WIDTH), BF),
        compiler_params=_params(("parallel", "parallel", "arbitrary")),
        name="na",
    )(p, p, p, p, p, bias)


def _cattn_kernel(q_ref, k_ref, v_ref, o_ref):
    q = (q_ref[...].astype(F32) * (NA_HEAD_DIM ** -0.5)).astype(BF)
    s = _dot_nt(q, k_ref[...])
    p = jnp.exp(s - jnp.max(s, axis=-1, keepdims=True))
    l = jnp.sum(p, axis=-1, keepdims=True)
    o_ref[...] = (_dot(p.astype(BF), v_ref[...]) / l).astype(o_ref.dtype)


def _cattn(p, *, n_batch, seq, ctx_len):
    hd = NA_HEAD_DIM
    blk0 = (n_batch * seq) // ctx_len
    return pl.pallas_call(
        _cattn_kernel,
        grid=(n_batch, NA_HEADS),
        in_specs=[pl.BlockSpec((ctx_len, hd), lambda b, h: (blk0 + b, COL_NA_Q // hd + h)),
                  pl.BlockSpec((ctx_len, hd), lambda b, h: (blk0 + b, COL_NA_K // hd + h)),
                  pl.BlockSpec((ctx_len, hd), lambda b, h: (blk0 + b, COL_NA_V // hd + h))],
        out_specs=pl.BlockSpec((ctx_len, hd), lambda b, h: (b, h)),
        out_shape=jax.ShapeDtypeStruct((n_batch * ctx_len, NA_WIDTH), BF),
        compiler_params=_params(("parallel", "parallel")),
        name="cattn",
    )(p, p, p)


def _bcast_rows(ref, idx_of_group, group, n_groups, lane0=0):
    return jnp.concatenate(
        [jnp.broadcast_to(ref[idx_of_group(g):idx_of_group(g) + 1, lane0:lane0 + LANES], (group, LANES))
         for g in range(n_groups)], axis=0)


def _masked(x, mask):
    return x if mask is None else x * mask


def _in_block_exact(q, k, cum, v_src, fwd, cum_ref, k_ref):
    cs = q.shape[0]
    nh = len(v_src)
    bs = SCAN_BLOCK
    nb = cs // bs
    n_half = bs // SUBLANES
    assert n_half == 2
    d_idx = lax.broadcasted_iota(jnp.int32, (LANES, LANES * nh), 0)
    j_idx = lax.broadcasted_iota(jnp.int32, (LANES, LANES * nh), 1)
    ones_mat = jnp.where(d_idx // (LANES // nh) == j_idx // LANES, 1.0, 0.0).astype(BF)

    def half(x, h):
        return jnp.concatenate([x[b * bs + h * SUBLANES:b * bs + (h + 1) * SUBLANES] for b in range(nb)],
                               axis=0)

    q_h = [half(q, h) for h in range(n_half)]
    c_h = [half(cum, h) for h in range(n_half)]
    tl = lax.broadcasted_iota(jnp.int32, (nb * SUBLANES, LANES), 0) & (SUBLANES - 1)
    acc = [[jnp.zeros((nb * SUBLANES, LANES), F32) for _ in range(n_half)] for _ in range(nh)]
    for s in range(bs):
        s_half, s_loc = divmod(s, SUBLANES)
        at = lambda b: b * bs + s
        c_s = _bcast_rows(cum_ref, at, SUBLANES, nb)
        k_s = _bcast_rows(k_ref, at, SUBLANES, nb)
        v_s = [_bcast_rows(ref, at, SUBLANES, nb, lane0) for ref, lane0 in v_src]
        halves = [h for h in range(n_half) if (h >= s_half if fwd else h <= s_half)]
        parts = []
        for h in halves:
            x = c_h[h] - c_s
            if h == s_half and s_loc != (0 if fwd else SUBLANES - 1):
                x = jnp.where(tl >= s_loc if fwd else tl <= s_loc, x, NEG)
            parts.append((q_h[h] * k_s * jnp.exp2(x)).astype(BF))
        r = _dot(jnp.concatenate(parts, axis=0), ones_mat)
        for i, h in enumerate(halves):
            rh = r[i * nb * SUBLANES:(i + 1) * nb * SUBLANES]
            for j in range(nh):
                acc[j][h] = acc[j][h] + rh[:, j * LANES:(j + 1) * LANES] * v_s[j]
    return [jnp.concatenate([acc[j][h][b * SUBLANES:(b + 1) * SUBLANES]
                             for b in range(nb) for h in range(n_half)], axis=0) for j in range(nh)]


def _in_block_factored(q, k, x0, vs, masks, fwd, block):
    cs = q.shape[0]
    row = lax.broadcasted_iota(jnp.int32, (cs, cs), 0)
    col = lax.broadcasted_iota(jnp.int32, (cs, cs), 1)
    ordered = row >= col if fwd else row <= col
    qf = q * jnp.exp2(x0)
    kf = (k * jnp.exp2(-x0)).astype(BF)
    outs = []
    for j in range(len(vs)):
        a = jnp.where(ordered, _dot_nt(_masked(qf, masks[j]).astype(BF), kf), 0.0)
        if block < cs:
            a = jnp.where((row ^ col) < block, a, 0.0)
        outs.append(_dot(a.astype(BF), vs[j].astype(BF)))
    return outs


def _levels(q, k, cum, vs, masks, fwd, cum_ref, m_lo, m_hi):
    cs = q.shape[0]
    nh = len(vs)
    bs = SCAN_BLOCK
    slabs = [[None] * (cs // bs) for _ in range(nh)]
    m = m_lo
    while m < m_hi:
        pair = 2 * m
        n_pairs = cs // pair
        late_off, early_off = (m, 0) if fwd else (0, m)
        bidx = m - 1 if fwd else m

        def take(x, off):
            return jnp.concatenate([x[p * pair + off:p * pair + off + m] for p in range(n_pairs)], axis=0)

        cb = _bcast_rows(cum_ref, lambda p: p * pair + bidx, m, n_pairs)
        qt = take(q, late_off) * jnp.exp2(take(cum, late_off) - cb)
        kt = (take(k, early_off) * jnp.exp2(cb - take(cum, early_off))).astype(BF)
        if n_pairs > 1:
            rh = lax.broadcasted_iota(jnp.int32, (cs // 2, cs // 2), 0)
            ch = lax.broadcasted_iota(jnp.int32, (cs // 2, cs // 2), 1)
            same_pair = (rh ^ ch) < m
        for j in range(nh):
            a = _dot_nt(_masked(qt, masks[j]).astype(BF), kt)
            if n_pairs > 1:
                a = jnp.where(same_pair, a, 0.0)
            o = _dot(a.astype(BF), take(vs[j], early_off).astype(BF))
            for p in range(n_pairs):
                for i in range(m // bs):
                    idx = (p * pair + late_off) // bs + i
                    piece = o[p * m + i * bs:p * m + (i + 1) * bs]
                    slabs[j][idx] = piece if slabs[j][idx] is None else slabs[j][idx] + piece
        m = pair
    return slabs


class _Chain:
    def __init__(self, scratch, c, nh):
        st_ref, cum_ref, k_ref, q_ref, rest_ref = scratch
        self.st = st_ref.at[c * nh:(c + 1) * nh]
        self.cum, self.k, self.q = cum_ref.at[c], k_ref.at[c], q_ref.at[c]
        self.rest = rest_ref.at[c * nh:(c + 1) * nh]


def _scan_chunk(q, k, lg, vs, v_src, masks, dirn, ch, write_out, fast_block, mid_block=None):
    cs = q.shape[0]
    nh = len(vs)
    bs = SCAN_BLOCK
    fwd = dirn == 0
    row = lax.broadcasted_iota(jnp.int32, (cs, cs), 0)
    col = lax.broadcasted_iota(jnp.int32, (cs, cs), 1)
    tri = jnp.where(row >= col if fwd else row <= col, 1.0, 0.0).astype(BF)
    hi = lg.astype(BF)
    r1 = lg - hi.astype(F32)
    mid = r1.astype(BF)
    lo = (r1 - mid.astype(F32)).astype(BF)
    cum = (_dot(tri, hi) + _dot(tri, mid) + _dot(tri, lo)) * LOG2E

    ch.cum[...] = cum
    ch.k[...] = k
    ch.q[...] = q

    first = 0 if fwd else fast_block - 1
    x0 = cum - _bcast_rows(ch.cum, lambda b: b * fast_block + first, fast_block, cs // fast_block)
    span = jnp.max(-x0)
    in_block = _in_block_factored(q, k, x0, vs, masks, fwd, fast_block)

    slabs = _levels(q, k, cum, vs, masks, fwd, ch.cum, fast_block, cs)

    last_row = cs - 1 if fwd else 0
    last = ch.cum[last_row:last_row + 1, :]
    qh = q * jnp.exp2(cum)
    kh = k * jnp.exp2(last - cum)
    dec = jnp.exp2(last)
    outs = []
    for j in range(nh):
        st = ch.st[j]
        vj = vs[j].astype(BF)
        rest = _join_slabs(slabs[j]) + _dot_nt(_masked(qh, masks[j]).astype(BF), st.astype(BF))
        ch.st[j] = st * dec + _dot_tn(vj, _masked(kh, masks[j]).astype(BF))
        ch.rest[j] = rest
        outs.append(rest + in_block[j])
    write_out(outs)

    def fixup():
        @pl.when(jnp.logical_not(span <= SAFE_LOG2_SPAN))
        def _():
            q_, k_, cum_ = ch.q[...], ch.k[...], ch.cum[...]
            vs_ = [ref[:, lane0:lane0 + LANES] for ref, lane0 in v_src]

            def rewrite(in_blk, lo):
                low = _levels(q_, k_, cum_, vs_, masks, fwd, ch.cum, lo, fast_block)
                write_out([ch.rest[j] + in_blk[j] + _join_slabs(low[j]) for j in range(nh)])

            def pairwise():
                rewrite(_in_block_exact(q_, k_, cum_, v_src, fwd, ch.cum, ch.k), bs)

            if mid_block is None:
                pairwise()
            else:
                first_m = 0 if fwd else mid_block - 1
                xm = cum_ - _bcast_rows(ch.cum, lambda b: b * mid_block + first_m, mid_block, cs // mid_block)
                span_m = jnp.max(-xm)
                pl.when(span_m <= SAFE_LOG2_SPAN)(
                    lambda: rewrite(_in_block_factored(q_, k_, xm, vs_, masks, fwd, mid_block), mid_block))
                pl.when(jnp.logical_not(span_m <= SAFE_LOG2_SPAN))(pairwise)

    return fixup


def _join_slabs(slabs):
    if all(sl is None for sl in slabs):
        return 0.0
    return jnp.concatenate([jnp.zeros((SCAN_BLOCK, LANES), F32) if sl is None else sl for sl in slabs],
                           axis=0)


def _hg_kernel(qf_ref, vf_ref, zf_ref, qb_ref, vb_ref, zb_ref, lbp_ref, of_ref, ob_ref, *scratch, layer):
    @pl.when(pl.program_id(2) == 0)
    def _():
        scratch[0][...] = jnp.zeros_like(scratch[0])

    blocks = (HG_MID_BLOCK, None) if layer == 0 else (HG_FAST_BLOCK, HG_MID_BLOCK)
    fixups = []
    for dirn, (q_ref, v_ref, z_ref, o_ref) in enumerate(((qf_ref, vf_ref, zf_ref, of_ref),
                                                         (qb_ref, vb_ref, zb_ref, ob_ref))):
        p = lbp_ref[dirn]
        e = jnp.exp(p - jnp.max(p, axis=0, keepdims=True))
        den = jnp.sum(e, axis=0, keepdims=True)
        lb_all = jnp.zeros_like(den)
        for i in range(1, layer + 1):
            lb_all = lb_all + e[i:i + 1]
        lb_all = lb_all / den
        for hh in range(HG_STEP_HEADS):
            lanes = slice(hh * LANES, (hh + 1) * LANES)
            lb = lb_all[:, lanes]
            z = z_ref[:, lanes]
            en = jnp.exp(-jnp.abs(z))
            big = 1.0 / (1.0 + en)
            small = en * big
            sig = jnp.where(z >= 0, big, small)
            nsig = jnp.where(z >= 0, small, big)
            k = (1.0 - lb) * nsig
            lg = jnp.log(lb + (1.0 - lb) * sig)
            qr = q_ref[:, lanes]
            q = qr * _sigmoid(qr)

            def write_out(outs, o_ref=o_ref, lanes=lanes):
                o_ref[:, lanes] = outs[0]

            ch = _Chain(scratch, dirn * HG_STEP_HEADS + hh, 1)
            fixups.append(_scan_chunk(q, k, lg, [v_ref[:, lanes]], [(v_ref, hh * LANES)], [None], dirn,
                                      ch, write_out, *blocks))
    for fixup in fixups:
        fixup()


def _gla_kernel(qf_ref, kf_ref, vf_ref, rf_ref, cosf_ref, sinf_ref,
                qb_ref, kb_ref, vb_ref, rb_ref, cosb_ref, sinb_ref, up_ref, gb_ref, of_ref, ob_ref, *scratch):
    @pl.when(pl.program_id(2) == 0)
    def _():
        scratch[0][...] = jnp.zeros_like(scratch[0])

    cs = qf_ref.shape[0]
    lane = lax.broadcasted_iota(jnp.int32, (cs, LANES), 1)
    first = (lane & 16) == 0
    lane1 = lax.broadcasted_iota(jnp.int32, (1, LANES), 1)
    masks = [jnp.where(lane1 < GLA_HEAD_K, 1.0, 0.0), jnp.where(lane1 >= GLA_HEAD_K, 1.0, 0.0)]
    fixups = []
    for dirn, (q_ref, k_ref, v_ref, r_ref, cos_ref, sin_ref, o_ref) in enumerate(
            ((qf_ref, kf_ref, vf_ref, rf_ref, cosf_ref, sinf_ref, of_ref),
             (qb_ref, kb_ref, vb_ref, rb_ref, cosb_ref, sinb_ref, ob_ref))):
        cos = cos_ref[...]
        sin = sin_ref[...]

        def rope(x):
            partner = jnp.where(first, pltpu.roll(x, LANES - 16, 1), pltpu.roll(x, 16, 1))
            return x * cos + partner * sin

        q = rope(q_ref[...] * (GLA_HEAD_K ** -0.5))
        k = rope(k_ref[...])
        pre = _dot(r_ref[...].astype(BF), up_ref[dirn].astype(BF)) + gb_ref[dirn]
        lg = (jnp.minimum(pre, 0.0) - jnp.log(1.0 + jnp.exp(-jnp.abs(pre)))) * (1.0 / GLA_GATE_NORM)

        def write_out(outs, o_ref=o_ref):
            o_ref[...] = jnp.concatenate(outs, axis=1)

        ch = _Chain(scratch, dirn, 2)
        fixups.append(_scan_chunk(q, k, lg, [v_ref[:, :LANES], v_ref[:, LANES:]],
                                  [(v_ref, 0), (v_ref, LANES)], masks, dirn, ch, write_out, GLA_FAST_BLOCK))
    for fixup in fixups:
        fixup()


HG_STEP_HEADS = 2


def _scan_scratch(n_chains, nh):
    cs = SCAN_CHUNK
    return [pltpu.VMEM((n_chains * nh, LANES, LANES), F32), pltpu.VMEM((n_chains, cs, LANES), F32),
            pltpu.VMEM((n_chains, cs, LANES), F32), pltpu.VMEM((n_chains, cs, LANES), F32),
            pltpu.VMEM((n_chains * nh, cs, LANES), F32)]


def _scan_rowblks(nc, ctx_blk0):
    fwd = lambda b, c: jnp.where(c == 0, ctx_blk0 + b, b * nc + c - 1)
    bwd = lambda b, c: jnp.where(c == 0, ctx_blk0 + b, b * nc + nc - c)
    return fwd, bwd


def _hg_scan(p, lbp, *, layer, n_batch, seq, n_all):
    cs = SCAN_CHUNK
    nc = seq // cs
    w = HG_STEP_HEADS * LANES
    rbs = _scan_rowblks(nc, (n_batch * seq) // cs)
    depth = lbp.shape[1]
    col = lambda c0: c0 // w

    def specs(rb, zcol):
        return [pl.BlockSpec((cs, w), lambda b, h, c: (rb(b, c), col(COL_HG_Q) + h)),
                pl.BlockSpec((cs, w), lambda b, h, c: (rb(b, c), col(COL_HG_I) + h)),
                pl.BlockSpec((cs, w), lambda b, h, c: (rb(b, c), col(zcol) + h))]

    return pl.pallas_call(
        functools.partial(_hg_kernel, layer=layer),
        grid=(n_batch, HG_HEADS // HG_STEP_HEADS, nc + 1),
        in_specs=specs(rbs[0], COL_HG_ZF) + specs(rbs[1], COL_HG_ZF + HG_WIDTH)
        + [pl.BlockSpec((2, depth, w), lambda b, h, c: (0, 0, h))],
        out_specs=[pl.BlockSpec((cs, w), lambda b, h, c: (rbs[0](b, c), h)),
                   pl.BlockSpec((cs, w), lambda b, h, c: (rbs[1](b, c), h))],
        out_shape=[jax.ShapeDtypeStruct((n_all, HG_WIDTH), F32)] * 2,
        scratch_shapes=_scan_scratch(2 * HG_STEP_HEADS, 1),
        compiler_params=_params(("parallel", "parallel", "arbitrary")),
        name="hg_scan",
    )(p, p, p, p, p, p, lbp)


def _gla_scan(p, r, up_pad, gate_b, cos_t, sin_t, *, n_batch, seq, n_all):
    cs = SCAN_CHUNK
    nc = seq // cs
    rbs = _scan_rowblks(nc, (n_batch * seq) // cs)
    tbs = (lambda b, c: jnp.where(c == 0, nc, c - 1), lambda b, c: jnp.where(c == 0, nc, nc - c))

    def specs(rb, tb):
        return [pl.BlockSpec((cs, LANES), lambda b, h, c: (rb(b, c), COL_GL_Q // LANES + h)),
                pl.BlockSpec((cs, LANES), lambda b, h, c: (rb(b, c), COL_GL_K // LANES + h)),
                pl.BlockSpec((cs, 2 * LANES), lambda b, h, c: (rb(b, c), COL_GL_V // (2 * LANES) + h)),
                pl.BlockSpec((cs, LANES), lambda b, h, c: (rb(b, c), 0)),
                pl.BlockSpec((cs, LANES), lambda b, h, c: (tb(b, c), 0)),
                pl.BlockSpec((cs, LANES), lambda b, h, c: (tb(b, c), 0))]

    return pl.pallas_call(
        _gla_kernel,
        grid=(n_batch, GLA_HEADS // 2, nc + 1),
        in_specs=specs(rbs[0], tbs[0]) + specs(rbs[1], tbs[1])
        + [pl.BlockSpec((2, LANES, LANES), lambda b, h, c: (0, 0, h)),
           pl.BlockSpec((2, 1, LANES), lambda b, h, c: (0, 0, h))],
        out_specs=[pl.BlockSpec((cs, 2 * LANES), lambda b, h, c: (rbs[0](b, c), h)),
                   pl.BlockSpec((cs, 2 * LANES), lambda b, h, c: (rbs[1](b, c), h))],
        out_shape=[jax.ShapeDtypeStruct((n_all, GLA_WIDTH), F32)] * 2,
        scratch_shapes=_scan_scratch(2, 2),
        compiler_params=_params(("parallel", "parallel", "arbitrary")),
        name="gla_scan",
    )(p, p, p, r, cos_t, sin_t, p, p, p, r, cos_t, sin_t, up_pad, gate_b)


def _rope_tables(seq, ctx_len):
    half = GLA_HEAD_K // 2
    inv = ROPE_BASE ** (-jnp.arange(0, half, 2, dtype=F32) / half)
    pos = jnp.arange(seq)
    ang_r = (pos // GRID_W).astype(F32)[:, None] * inv
    ang_c = (pos % GRID_W).astype(F32)[:, None] * inv
    cos = jnp.concatenate([jnp.cos(ang_r)] * 2 + [jnp.cos(ang_c)] * 2, axis=1)
    sin = jnp.concatenate([-jnp.sin(ang_r), jnp.sin(ang_r), -jnp.sin(ang_c), jnp.sin(ang_c)], axis=1)
    cos = jnp.concatenate([jnp.tile(cos, (1, 2)), jnp.ones((ctx_len, LANES), F32)], axis=0)
    sin = jnp.concatenate([jnp.tile(sin, (1, 2)), jnp.zeros((ctx_len, LANES), F32)], axis=0)
    return cos, sin


def _outproj_kernel(na_ref, nc_ref, hgf_ref, hgb_ref, ghg_ref, glf_ref, glb_ref, ggl_ref, nhg_ref, ngl_ref,
                    w_ref, x_ref, gt_ref, sh_ref, sc_ref, n2_ref, x1_ref, h2_ref, edge_ref, mix_ref, *,
                    n_lat_tiles):
    @pl.when(pl.program_id(0) < n_lat_tiles)
    def _():
        mix_ref[:, :NA_WIDTH] = na_ref[...]

    @pl.when(pl.program_id(0) >= n_lat_tiles)
    def _():
        mix_ref[:, :NA_WIDTH] = nc_ref[...]

    def norm_gate(of_ref, ob_ref, g_ref, gain_ref, col0, n_heads):
        o = of_ref[...] + ob_ref[...]
        g = g_ref[...]
        for h in range(n_heads):
            sl = slice(h * LANES, (h + 1) * LANES)
            gh = g[:, sl]
            y = _rms(o[:, sl]) * gain_ref[...]
            mix_ref[:, col0 + h * LANES:col0 + (h + 1) * LANES] = (y * (gh * _sigmoid(gh))).astype(BF)

    norm_gate(hgf_ref, hgb_ref, ghg_ref, nhg_ref, NA_WIDTH, HG_HEADS)
    norm_gate(glf_ref, glb_ref, ggl_ref, ngl_ref, NA_WIDTH + HG_WIDTH, GLA_HEADS)
    x1 = x_ref[...] + gt_ref[0] * _dot(mix_ref[...], w_ref[0])
    x1_ref[...] = x1
    y = _rms(x1) * n2_ref[...]
    h2 = y * (1.0 + sc_ref[0]) + sh_ref[0]
    h2_ref[...] = h2.astype(BF)
    edge_ref[0] = jnp.concatenate([h2[:1], h2[-1:]], axis=0)


OUT_TM = 512


def _outproj(na_o, na_ctx, o_hg, o_gl, p, hg_gain, gl_gain, w_out, x_all, mod, norm2_g, *,
             layer, n_rows, seq, n_batch):
    d = D_MODEL
    tm = OUT_TM
    tpb = seq // tm
    n_lat_tiles = n_batch * tpb
    mod_row = lambda k: (lambda i: (jnp.minimum(i // tpb, n_batch) * 6 + k, 0, 0))
    return pl.pallas_call(
        functools.partial(_outproj_kernel, n_lat_tiles=n_lat_tiles),
        grid=(n_rows // tm,),
        in_specs=[pl.BlockSpec((tm, NA_WIDTH), lambda i: (jnp.minimum(i, n_lat_tiles - 1), 0)),
                  pl.BlockSpec((tm, NA_WIDTH), lambda i: (jnp.maximum(i - n_lat_tiles, 0), 0)),
                  pl.BlockSpec((tm, HG_WIDTH), lambda i: (i, 0)),
                  pl.BlockSpec((tm, HG_WIDTH), lambda i: (i, 0)),
                  pl.BlockSpec((tm, HG_WIDTH), lambda i: (i, COL_HG_G // HG_WIDTH)),
                  pl.BlockSpec((tm, GLA_WIDTH), lambda i: (i, 0)),
                  pl.BlockSpec((tm, GLA_WIDTH), lambda i: (i, 0)),
                  pl.BlockSpec((tm, GLA_WIDTH), lambda i: (i, COL_GL_G // GLA_WIDTH)),
                  pl.BlockSpec((1, LANES), lambda i: (0, 0)),
                  pl.BlockSpec((1, LANES), lambda i: (0, 0)),
                  pl.BlockSpec((1, d, d), lambda i: (layer, 0, 0), pipeline_mode=pl.Buffered(1)),
                  pl.BlockSpec((tm, d), lambda i: (i, 0)),
                  pl.BlockSpec((1, 1, d), mod_row(2)),
                  pl.BlockSpec((1, 1, d), mod_row(3)),
                  pl.BlockSpec((1, 1, d), mod_row(4)),
                  pl.BlockSpec((1, d), lambda i: (0, 0))],
        out_specs=[pl.BlockSpec((tm, d), lambda i: (i, 0)),
                   pl.BlockSpec((tm, d), lambda i: (i, 0)),
                   pl.BlockSpec((1, 2, d), lambda i: (i, 0, 0))],
        out_shape=[jax.ShapeDtypeStruct((n_rows, d), F32),
                   jax.ShapeDtypeStruct((n_rows, d), BF),
                   jax.ShapeDtypeStruct((n_rows // tm, 2, d), F32)],
        scratch_shapes=[pltpu.VMEM((tm, d), BF)],
        compiler_params=_params(("parallel",)),
        name="outproj",
    )(na_o, na_ctx, o_hg[0], o_hg[1], p, o_gl[0], o_gl[1], p, hg_gain, gl_gain, w_out, x_all, mod, mod, mod,
      norm2_g)


def _patch_row(x, r, row_in_slab, value, keep=None):
    r0 = (r // SUBLANES) * SUBLANES
    slab = x[r0:r0 + SUBLANES]
    new = value if keep is None else slab * keep + value * (1.0 - keep)
    slab = jnp.where(row_in_slab == r - r0, new, slab)
    parts = [x[:r0]] * (r0 > 0) + [slab] + [x[r0 + SUBLANES:]] * (r0 + SUBLANES < x.shape[0])
    return jnp.concatenate(parts, axis=0)


def _ffn_kernel(h_ref, halo_ref, wa_ref, wb_ref, cwa_ref, cwb_ref, cba_ref, cbb_ref, wd_ref,
                x1_ref, gt_ref, fg_ref, o_ref, acc_ref, hc_ref, *, n_lat_tiles, ctx_len, final_norm):
    i = pl.program_id(0)
    j = pl.program_id(1)
    tm = h_ref.shape[0]
    tf = wa_ref.shape[-1]

    @pl.when(j == 0)
    def _():
        acc_ref[...] = jnp.zeros_like(acc_ref)
        hc_ref[:tm] = h_ref[...]
        hc_ref[tm:] = halo_ref[0]

    hc = hc_ref[...]
    row8 = lax.broadcasted_iota(jnp.int32, (SUBLANES, tf), 0)
    keep_lat = jnp.where(i < n_lat_tiles, 1.0, 0.0)
    zero_row = jnp.zeros((1, tf), F32)

    def conv(w_ref, cw_ref, cb_ref):
        u_all = _dot(hc, w_ref[0])
        u = u_all[:tm]
        prev = _patch_row(pltpu.roll(u, 1, 0), 0, row8, u_all[tm:tm + 1])
        nxt = _patch_row(pltpu.roll(u, tm - 1, 0), tm - 1, row8, u_all[tm + 1:tm + 2])
        for r in range(ctx_len, tm, ctx_len):
            prev = _patch_row(prev, r, row8, zero_row, keep_lat)
            nxt = _patch_row(nxt, r - 1, row8, zero_row, keep_lat)
        cw = cw_ref[...]
        return cw[0:1] * prev + cw[1:2] * u + cw[2:3] * nxt + cb_ref[...]

    a = conv(wa_ref, cwa_ref, cba_ref)
    b = conv(wb_ref, cwb_ref, cbb_ref)
    g = (a * _sigmoid(a) * b).astype(BF)
    acc_ref[...] += _dot(g, wd_ref[0])

    @pl.when(j == pl.num_programs(1) - 1)
    def _():
        x2 = x1_ref[...] + gt_ref[0] * acc_ref[...]
        if final_norm:
            x2 = _rms(x2) * fg_ref[...]
        o_ref[...] = x2


FFN_TM = 512
FFN_HALO = 2 * SUBLANES


def _ffn(h2, halo, w_up, conv_w, conv_b, w_down, x1, mod, final_g, *, layer, n_rows, n_lat_tiles,
         seq, n_batch, ctx_len, final_norm):
    d = D_MODEL
    tm, tf = FFN_TM, 512
    nf = D_FF // tf
    tpb = seq // tm
    mod_row = lambda i, j: (jnp.minimum(i // tpb, n_batch) * 6 + 5, 0, 0)
    return pl.pallas_call(
        functools.partial(_ffn_kernel, n_lat_tiles=n_lat_tiles, ctx_len=ctx_len,
                          final_norm=final_norm),
        grid=(n_rows // tm, nf),
        in_specs=[pl.BlockSpec((tm, d), lambda i, j: (i, 0)),
                  pl.BlockSpec((1, FFN_HALO, d), lambda i, j: (i, 0, 0)),
                  pl.BlockSpec((1, d, tf), lambda i, j: (layer, 0, j)),
                  pl.BlockSpec((1, d, tf), lambda i, j: (layer, 0, nf + j)),
                  pl.BlockSpec((3, tf), lambda i, j: (0, j)),
                  pl.BlockSpec((3, tf), lambda i, j: (0, nf + j)),
                  pl.BlockSpec((1, tf), lambda i, j: (0, j)),
                  pl.BlockSpec((1, tf), lambda i, j: (0, nf + j)),
                  pl.BlockSpec((1, tf, d), lambda i, j: (layer, j, 0)),
                  pl.BlockSpec((tm, d), lambda i, j: (i, 0)),
                  pl.BlockSpec((1, 1, d), mod_row),
                  pl.BlockSpec((1, d), lambda i, j: (0, 0))],
        out_specs=pl.BlockSpec((tm, d), lambda i, j: (i, 0)),
        out_shape=jax.ShapeDtypeStruct((n_rows, d), F32),
        scratch_shapes=[pltpu.VMEM((tm, d), F32), pltpu.VMEM((tm + FFN_HALO, d), BF)],
        compiler_params=_params(("parallel", "arbitrary")),
        name="ffn",
    )(h2, halo, w_up, w_up, conv_w, conv_w, conv_b, conv_b, w_down, x1, mod, final_g)


def _ffn_halo(edges, n_tiles, n_lat, seq, ctx_len):
    k = FFN_TM // OUT_TM
    d = edges.shape[-1]
    starts = np.arange(n_tiles) * FFN_TM
    seq_of = np.where(starts < n_lat, seq, ctx_len)
    rel = np.where(starts < n_lat, starts, starts - n_lat)
    prev_ok = jnp.asarray((rel % seq_of != 0).astype(np.float32))[:, None]
    next_ok = jnp.asarray(((rel + FFN_TM) % seq_of != 0).astype(np.float32))[:, None]
    zero = jnp.zeros((1, d), edges.dtype)
    last_rows = edges[k - 1::k, 1][:n_tiles]
    first_rows = edges[0::k, 0][:n_tiles]
    prev = jnp.concatenate([zero, last_rows[:-1]], axis=0) * prev_ok
    nxt = jnp.concatenate([first_rows[1:], zero], axis=0) * next_ok
    pad = jnp.zeros((n_tiles, FFN_HALO - 2, d), edges.dtype)
    return jnp.concatenate([prev[:, None], nxt[:, None], pad], axis=1).astype(BF)


def kernel(x, c, ctx, c_ctx, ada_w, ada_b, norm1_g, w_in, na_rpb, hg_lower_bounds, hg_norm_g,
           gla_gate_up, gla_gate_b, gla_norm_g, w_out, norm2_g, w_up, conv_w, conv_b, w_down, final_g):
    n_batch, seq, d = x.shape
    ctx_len = ctx.shape[1]
    depth = ada_w.shape[0]
    n_lat = n_batch * seq
    n_all = n_lat + n_batch * ctx_len
    assert d == D_MODEL and ctx_len == SCAN_CHUNK and seq % (NA_QROWS * GRID_W) == 0
    assert seq // GRID_W >= NA_KROWS + NA_QROWS and n_batch < SUBLANES

    x_all = jnp.concatenate([x.reshape(n_lat, d), ctx.reshape(n_batch * ctx_len, d)], axis=0)
    c_all = jnp.concatenate([c, c_ctx[None], jnp.zeros((SUBLANES - n_batch - 1, d), F32)], axis=0)
    mods = _ada(c_all, ada_w, ada_b)
    cos_t, sin_t = _rope_tables(seq, ctx_len)
    w_in_b, w_out_b, w_up_b, w_down_b = (w.astype(BF) for w in (w_in, w_out, w_up, w_down))

    for l in range(depth):
        last = l == depth - 1
        mod = mods[l].reshape(SUBLANES * 6, 1, d)
        w_rank = jnp.pad(w_in[l, :, MAIN_COLS:], ((0, 0), (0, LANES - 2 * GLA_RANK))).astype(BF)
        p_na, p, r = _inproj(x_all, norm1_g[l][None], mod, w_in_b, w_rank, layer=l, seq=seq, n_batch=n_batch)

        bias = _na_bias_table(na_rpb[l], seq // GRID_W)
        na_o = _na(p_na, bias, n_batch=n_batch, seq=seq, ctx_len=ctx_len)
        na_ctx = na_o if last else _cattn(p_na, n_batch=n_batch, seq=seq, ctx_len=ctx_len)

        o_hg = _hg_scan(p, hg_lower_bounds, layer=l, n_batch=n_batch, seq=seq, n_all=n_all)
        up = gla_gate_up[l]
        up_pad = jnp.zeros((2, LANES, GLA_KEY_WIDTH), F32)
        up_pad = up_pad.at[0, :GLA_RANK].set(up[0]).at[1, GLA_RANK:2 * GLA_RANK].set(up[1])
        o_gl = _gla_scan(p, r, up_pad, gla_gate_b[l][:, None, :], cos_t, sin_t,
                         n_batch=n_batch, seq=seq, n_all=n_all)

        n_rows = n_lat if last else n_all
        x1, h2, edges = _outproj(na_o, na_ctx, o_hg, o_gl, p, hg_norm_g[l][None], gla_norm_g[l][None],
                                 w_out_b, x_all, mod, norm2_g[l][None], layer=l, n_rows=n_rows,
                                 seq=seq, n_batch=n_batch)
        halo = _ffn_halo(edges, n_rows // FFN_TM, n_lat, seq, ctx_len)
        x_all = _ffn(h2, halo, w_up_b, conv_w[l], conv_b[l][None], w_down_b,
                     x1, mod, final_g[None], layer=l, n_rows=n_rows, n_lat_tiles=n_lat // FFN_TM,
                     seq=seq, n_batch=n_batch, ctx_len=ctx_len, final_norm=last)
    return x_all[:n_lat].reshape(n_batch, seq, d)
```

```python
import functools

import numpy as np
import jax
import jax.numpy as jnp
from jax import lax
from jax.experimental import pallas as pl
from jax.experimental.pallas import tpu as pltpu

F32 = jnp.float32
BF = jnp.bfloat16

D_MODEL = 2048
GRID_W = 64
NA_HEADS = 8
NA_HEAD_DIM = 128
NA_WIDTH = NA_HEADS * NA_HEAD_DIM
NA_ROWS = 8
NA_COLS = 16
HG_WIDTH = 512
HG_HEADS = 4
GLA_WIDTH = 512
GLA_HEADS = 4
GLA_KEY_WIDTH = 256
GLA_HEAD_K = 64
GLA_RANK = 16
GLA_GATE_NORM = 16.0
D_FF = 5632
ROPE_BASE = 10000.0
EPS = 1e-6
MAIN_COLS = 7168
NA_COLS_ALL = 3 * NA_WIDTH
COL_NA_Q, COL_NA_K, COL_NA_V = 0, 1024, 2048
COL_HG_Q, COL_HG_I, COL_HG_ZF, COL_HG_G = 0, 512, 1024, 2048
COL_GL_Q, COL_GL_K, COL_GL_V, COL_GL_G = 2560, 2816, 3072, 3584

LANES = 128
SUBLANES = 8
VMEM_LIMIT = 56 * 1024 * 1024

NA_QROWS = 8
NA_KROWS = 16
SCAN_CHUNK = 256
SCAN_BLOCK = 16
NEG = -1e30
LOG2E = 1.4426950408889634
SAFE_LOG2_SPAN = 96.0
HG_FAST_BLOCK = 32
HG_MID_BLOCK = 16
GLA_FAST_BLOCK = SCAN_CHUNK


def _dot(a, b):
    return jnp.dot(a, b, preferred_element_type=F32)


def _dot_nt(a, b):
    return lax.dot_general(a, b, (((1,), (1,)), ((), ())), preferred_element_type=F32)


def _dot_tn(a, b):
    return lax.dot_general(a, b, (((0,), (0,)), ((), ())), preferred_element_type=F32)


def _sigmoid(x):
    return 1.0 / (1.0 + jnp.exp(-x))


def _rms(x):
    return x * lax.rsqrt(jnp.mean(x * x, axis=-1, keepdims=True) + EPS)


def _params(sem):
    return pltpu.CompilerParams(dimension_semantics=sem, vmem_limit_bytes=VMEM_LIMIT)


def _ada_kernel(c_ref, w_ref, b_ref, o_ref):
    c = c_ref[...]
    s = (c * _sigmoid(c)).astype(BF)
    o_ref[0] = _dot(s, w_ref[0].astype(BF)) + b_ref[0]


def _ada(c_all, ada_w, ada_b):
    depth, d, n = ada_w.shape
    tn = 1024
    return pl.pallas_call(
        _ada_kernel,
        grid=(depth, n // tn),
        in_specs=[pl.BlockSpec((SUBLANES, d), lambda l, j: (0, 0)),
                  pl.BlockSpec((1, d, tn), lambda l, j: (l, 0, j)),
                  pl.BlockSpec((1, 1, tn), lambda l, j: (l, 0, j))],
        out_specs=pl.BlockSpec((1, SUBLANES, tn), lambda l, j: (l, 0, j)),
        out_shape=jax.ShapeDtypeStruct((depth, SUBLANES, n), F32),
        compiler_params=_params(("parallel", "parallel")),
        name="ada",
    )(c_all, ada_w, ada_b.reshape(depth, 1, n))


def _inproj_kernel(x_ref, g_ref, sh_ref, sc_ref, w_ref, wr_ref, pna_ref, prest_ref, r_ref, h_ref,
                   *, n_na_tiles):
    j = pl.program_id(1)

    @pl.when(j == 0)
    def _():
        y = _rms(x_ref[...]) * g_ref[...]
        h = (y * (1.0 + sc_ref[0]) + sh_ref[0]).astype(BF)
        h_ref[...] = h
        r_ref[...] = _dot(h, wr_ref[...])

    acc = _dot(h_ref[...], w_ref[0])

    @pl.when(j < n_na_tiles)
    def _():
        pna_ref[...] = acc.astype(BF)

    @pl.when(j >= n_na_tiles)
    def _():
        prest_ref[...] = acc


def _inproj(x_all, norm_g, mod, w_in, w_rank, *, layer, seq, n_batch):
    n, d = x_all.shape
    tm, tn = 1024, 1024
    n_na = NA_COLS_ALL // tn
    tpb = seq // tm
    mod_row = lambda k: (lambda i, j: (jnp.minimum(i // tpb, n_batch) * 6 + k, 0, 0))
    return pl.pallas_call(
        functools.partial(_inproj_kernel, n_na_tiles=n_na),
        grid=(n // tm, MAIN_COLS // tn),
        in_specs=[pl.BlockSpec((tm, d), lambda i, j: (i, 0)),
                  pl.BlockSpec((1, d), lambda i, j: (0, 0)),
                  pl.BlockSpec((1, 1, d), mod_row(0)),
                  pl.BlockSpec((1, 1, d), mod_row(1)),
                  pl.BlockSpec((1, d, tn), lambda i, j: (layer, 0, j)),
                  pl.BlockSpec((d, LANES), lambda i, j: (0, 0))],
        out_specs=[pl.BlockSpec((tm, tn), lambda i, j: (i, jnp.minimum(j, n_na - 1))),
                   pl.BlockSpec((tm, tn), lambda i, j: (i, jnp.maximum(j - n_na, 0))),
                   pl.BlockSpec((tm, LANES), lambda i, j: (i, 0))],
        out_shape=[jax.ShapeDtypeStruct((n, NA_COLS_ALL), BF),
                   jax.ShapeDtypeStruct((n, MAIN_COLS - NA_COLS_ALL), F32),
                   jax.ShapeDtypeStruct((n, LANES), F32)],
        scratch_shapes=[pltpu.VMEM((tm, d), BF)],
        compiler_params=_params(("parallel", "arbitrary")),
        name="inproj",
    )(x_all, norm_g, mod, mod, w_in, w_rank)


def _na_bias_table(rpb, rows):
    n_heads = rpb.shape[0]
    n_dr = 2 * NA_ROWS - 1
    qc = np.arange(GRID_W)[:, None]
    kc = np.arange(GRID_W)[None, :]
    cstart = np.clip(qc - NA_COLS // 2, 0, GRID_W - NA_COLS)
    col_ok = (kc >= cstart) & (kc < cstart + NA_COLS)
    dc_idx = np.clip(kc - qc, -(NA_COLS - 1), NA_COLS - 1) + NA_COLS - 1
    onehot = (dc_idx[None] == np.arange(2 * NA_COLS - 1)[:, None, None]).astype(np.float32)
    by_col = jnp.einsum('hrj,jqk->hqrk', rpb, jnp.asarray(onehot), precision=lax.Precision.HIGHEST)
    by_col = jnp.where(jnp.asarray(col_ok)[None, :, None, :], by_col, NEG)
    pad = NA_KROWS
    flat = jnp.pad(by_col, ((0, 0), (0, 0), (pad, pad), (0, 0))).reshape(n_heads, GRID_W, -1)
    kr = np.arange(NA_KROWS)
    cases = []
    for r0 in (0, NA_QROWS, rows - NA_QROWS):
        ks = int(np.clip(r0 - NA_ROWS // 2, 0, rows - NA_KROWS))
        per_row = []
        for qr in range(NA_QROWS):
            r = r0 + qr
            rs = int(np.clip(r - NA_ROWS // 2, 0, rows - NA_ROWS))
            row_ok = np.repeat((ks + kr >= rs) & (ks + kr < rs + NA_ROWS), GRID_W)
            lo = ks - r + NA_ROWS - 1 + pad
            assert 0 <= lo and lo + NA_KROWS <= n_dr + 2 * pad
            sl = flat[:, :, lo * GRID_W:(lo + NA_KROWS) * GRID_W]
            per_row.append(jnp.where(jnp.asarray(row_ok)[None, None, :], sl, NEG))
        cases.append(jnp.stack(per_row, axis=1).reshape(n_heads, NA_QROWS * GRID_W, NA_KROWS * GRID_W))
    return jnp.stack(cases, axis=1)


def _na_kernel(q_ref, k_ref, v_ref, kc_ref, vc_ref, bias_ref, o_ref, *, rows):
    rb = pl.program_id(2)
    nk = NA_KROWS * GRID_W
    ks = jnp.clip(rb * NA_QROWS - NA_ROWS // 2, 0, rows - NA_KROWS) * GRID_W
    ks = pl.multiple_of(ks, GRID_W)
    for hh in range(NA_STEP_HEADS):
        lanes = slice(hh * NA_HEAD_DIM, (hh + 1) * NA_HEAD_DIM)
        kblk = k_ref[pl.ds(ks, nk), lanes]
        vblk = v_ref[pl.ds(ks, nk), lanes]
        q = (q_ref[:, lanes].astype(F32) * (NA_HEAD_DIM ** -0.5)).astype(BF)
        s_loc = _dot_nt(q, kblk) + bias_ref[hh, 0]
        s_ctx = _dot_nt(q, kc_ref[:, lanes])
        m = jnp.maximum(jnp.max(s_loc, axis=-1, keepdims=True), jnp.max(s_ctx, axis=-1, keepdims=True))
        p_loc = jnp.exp(s_loc - m)
        p_ctx = jnp.exp(s_ctx - m)
        l = jnp.sum(p_loc, axis=-1, keepdims=True) + jnp.sum(p_ctx, axis=-1, keepdims=True)
        o = _dot(p_loc.astype(BF), vblk) + _dot(p_ctx.astype(BF), vc_ref[:, lanes])
        o_ref[:, lanes] = (o / l).astype(o_ref.dtype)


NA_STEP_HEADS = 2


def _na(p, bias, *, n_batch, seq, ctx_len):
    rows = seq // GRID_W
    tq = NA_QROWS * GRID_W
    rbs = seq // tq
    w = NA_STEP_HEADS * NA_HEAD_DIM
    ctx_blk0 = (n_batch * seq) // ctx_len
    case = lambda rb: jnp.where(rb == 0, 0, jnp.where(rb == rbs - 1, 2, 1))
    return pl.pallas_call(
        functools.partial(_na_kernel, rows=rows),
        grid=(n_batch, NA_HEADS // NA_STEP_HEADS, rbs),
        in_specs=[pl.BlockSpec((tq, w), lambda b, h, r: (b * rbs + r, COL_NA_Q // w + h)),
                  pl.BlockSpec((seq, w), lambda b, h, r: (b, COL_NA_K // w + h)),
                  pl.BlockSpec((seq, w), lambda b, h, r: (b, COL_NA_V // w + h)),
                  pl.BlockSpec((ctx_len, w), lambda b, h, r: (ctx_blk0 + b, COL_NA_K // w + h)),
                  pl.BlockSpec((ctx_len, w), lambda b, h, r: (ctx_blk0 + b, COL_NA_V // w + h)),
                  pl.BlockSpec((NA_STEP_HEADS, 1, tq, NA_KROWS * GRID_W), lambda b, h, r: (h, case(r), 0, 0))],
        out_specs=pl.BlockSpec((tq, w), lambda b, h, r: (b * rbs + r, h)),
        out_shape=jax.ShapeDtypeStruct((n_batch * seq, NA_WIDTH), BF),
        compiler_params=_params(("parallel", "parallel", "arbitrary")),
        name="na",
    )(p, p, p, p, p, bias)


def _cattn_kernel(q_ref, k_ref, v_ref, o_ref):
    q = (q_ref[...].astype(F32) * (NA_HEAD_DIM ** -0.5)).astype(BF)
    s = _dot_nt(q, k_ref[...])
    p = jnp.exp(s - jnp.max(s, axis=-1, keepdims=True))
    l = jnp.sum(p, axis=-1, keepdims=True)
    o_ref[...] = (_dot(p.astype(BF), v_ref[...]) / l).astype(o_ref.dtype)


def _cattn(p, *, n_batch, seq, ctx_len):
    hd = NA_HEAD_DIM
    blk0 = (n_batch * seq) // ctx_len
    return pl.pallas_call(
        _cattn_kernel,
        grid=(n_batch, NA_HEADS),
        in_specs=[pl.BlockSpec((ctx_len, hd), lambda b, h: (blk0 + b, COL_NA_Q // hd + h)),
                  pl.BlockSpec((ctx_len, hd), lambda b, h: (blk0 + b, COL_NA_K // hd + h)),
                  pl.BlockSpec((ctx_len, hd), lambda b, h: (blk0 + b, COL_NA_V // hd + h))],
        out_specs=pl.BlockSpec((ctx_len, hd), lambda b, h: (b, h)),
        out_shape=jax.ShapeDtypeStruct((n_batch * ctx_len, NA_WIDTH), BF),
        compiler_params=_params(("parallel", "parallel")),
        name="cattn",
    )(p, p, p)


def _bcast_rows(ref, idx_of_group, group, n_groups, lane0=0):
    return jnp.concatenate(
        [jnp.broadcast_to(ref[idx_of_group(g):idx_of_group(g) + 1, lane0:lane0 + LANES], (group, LANES))
         for g in range(n_groups)], axis=0)


def _masked(x, mask):
    return x if mask is None else x * mask


def _in_block_exact(q, k, cum, v_src, fwd, cum_ref, k_ref):
    cs = q.shape[0]
    nh = len(v_src)
    bs = SCAN_BLOCK
    nb = cs // bs
    n_half = bs // SUBLANES
    assert n_half == 2
    d_idx = lax.broadcasted_iota(jnp.int32, (LANES, LANES * nh), 0)
    j_idx = lax.broadcasted_iota(jnp.int32, (LANES, LANES * nh), 1)
    ones_mat = jnp.where(d_idx // (LANES // nh) == j_idx // LANES, 1.0, 0.0).astype(BF)

    def half(x, h):
        return jnp.concatenate([x[b * bs + h * SUBLANES:b * bs + (h + 1) * SUBLANES] for b in range(nb)],
                               axis=0)

    q_h = [half(q, h) for h in range(n_half)]
    c_h = [half(cum, h) for h in range(n_half)]
    tl = lax.broadcasted_iota(jnp.int32, (nb * SUBLANES, LANES), 0) & (SUBLANES - 1)
    acc = [[jnp.zeros((nb * SUBLANES, LANES), F32) for _ in range(n_half)] for _ in range(nh)]
    for s in range(bs):
        s_half, s_loc = divmod(s, SUBLANES)
        at = lambda b: b * bs + s
        c_s = _bcast_rows(cum_ref, at, SUBLANES, nb)
        k_s = _bcast_rows(k_ref, at, SUBLANES, nb)
        v_s = [_bcast_rows(ref, at, SUBLANES, nb, lane0) for ref, lane0 in v_src]
        halves = [h for h in range(n_half) if (h >= s_half if fwd else h <= s_half)]
        parts = []
        for h in halves:
            x = c_h[h] - c_s
            if h == s_half and s_loc != (0 if fwd else SUBLANES - 1):
                x = jnp.where(tl >= s_loc if fwd else tl <= s_loc, x, NEG)
            parts.append((q_h[h] * k_s * jnp.exp2(x)).astype(BF))
        r = _dot(jnp.concatenate(parts, axis=0), ones_mat)
        for i, h in enumerate(halves):
            rh = r[i * nb * SUBLANES:(i + 1) * nb * SUBLANES]
            for j in range(nh):
                acc[j][h] = acc[j][h] + rh[:, j * LANES:(j + 1) * LANES] * v_s[j]
    return [jnp.concatenate([acc[j][h][b * SUBLANES:(b + 1) * SUBLANES]
                             for b in range(nb) for h in range(n_half)], axis=0) for j in range(nh)]


def _in_block_factored(q, k, x0, vs, masks, fwd, block):
    cs = q.shape[0]
    row = lax.broadcasted_iota(jnp.int32, (cs, cs), 0)
    col = lax.broadcasted_iota(jnp.int32, (cs, cs), 1)
    ordered = row >= col if fwd else row <= col
    qf = q * jnp.exp2(x0)
    kf = (k * jnp.exp2(-x0)).astype(BF)
    outs = []
    for j in range(len(vs)):
        a = jnp.where(ordered, _dot_nt(_masked(qf, masks[j]).astype(BF), kf), 0.0)
        if block < cs:
            a = jnp.where((row ^ col) < block, a, 0.0)
        outs.append(_dot(a.astype(BF), vs[j].astype(BF)))
    return outs


def _levels(q, k, cum, vs, masks, fwd, cum_ref, m_lo, m_hi):
    cs = q.shape[0]
    nh = len(vs)
    bs = SCAN_BLOCK
    slabs = [[None] * (cs // bs) for _ in range(nh)]
    m = m_lo
    while m < m_hi:
        pair = 2 * m
        n_pairs = cs // pair
        late_off, early_off = (m, 0) if fwd else (0, m)
        bidx = m - 1 if fwd else m

        def take(x, off):
            return jnp.concatenate([x[p * pair + off:p * pair + off + m] for p in range(n_pairs)], axis=0)

        cb = _bcast_rows(cum_ref, lambda p: p * pair + bidx, m, n_pairs)
        qt = take(q, late_off) * jnp.exp2(take(cum, late_off) - cb)
        kt = (take(k, early_off) * jnp.exp2(cb - take(cum, early_off))).astype(BF)
        if n_pairs > 1:
            rh = lax.broadcasted_iota(jnp.int32, (cs // 2, cs // 2), 0)
            ch = lax.broadcasted_iota(jnp.int32, (cs // 2, cs // 2), 1)
            same_pair = (rh ^ ch) < m
        for j in range(nh):
            a = _dot_nt(_masked(qt, masks[j]).astype(BF), kt)
            if n_pairs > 1:
                a = jnp.where(same_pair, a, 0.0)
            o = _dot(a.astype(BF), take(vs[j], early_off).astype(BF))
            for p in range(n_pairs):
                for i in range(m // bs):
                    idx = (p * pair + late_off) // bs + i
                    piece = o[p * m + i * bs:p * m + (i + 1) * bs]
                    slabs[j][idx] = piece if slabs[j][idx] is None else slabs[j][idx] + piece
        m = pair
    return slabs


class _Chain:
    def __init__(self, scratch, c, nh):
        st_ref, cum_ref, k_ref, q_ref, rest_ref = scratch
        self.st = st_ref.at[c * nh:(c + 1) * nh]
        self.cum, self.k, self.q = cum_ref.at[c], k_ref.at[c], q_ref.at[c]
        self.rest = rest_ref.at[c * nh:(c + 1) * nh]


def _scan_chunk(q, k, lg, vs, v_src, masks, dirn, ch, write_out, fast_block, mid_block=None):
    cs = q.shape[0]
    nh = len(vs)
    bs = SCAN_BLOCK
    fwd = dirn == 0
    row = lax.broadcasted_iota(jnp.int32, (cs, cs), 0)
    col = lax.broadcasted_iota(jnp.int32, (cs, cs), 1)
    tri = jnp.where(row >= col if fwd else row <= col, 1.0, 0.0).astype(BF)
    hi = lg.astype(BF)
    r1 = lg - hi.astype(F32)
    mid = r1.astype(BF)
    lo = (r1 - mid.astype(F32)).astype(BF)
    cum = (_dot(tri, hi) + _dot(tri, mid) + _dot(tri, lo)) * LOG2E

    ch.cum[...] = cum
    ch.k[...] = k
    ch.q[...] = q

    first = 0 if fwd else fast_block - 1
    x0 = cum - _bcast_rows(ch.cum, lambda b: b * fast_block + first, fast_block, cs // fast_block)
    span = jnp.max(-x0)
    in_block = _in_block_factored(q, k, x0, vs, masks, fwd, fast_block)

    slabs = _levels(q, k, cum, vs, masks, fwd, ch.cum, fast_block, cs)

    last_row = cs - 1 if fwd else 0
    last = ch.cum[last_row:last_row + 1, :]
    qh = q * jnp.exp2(cum)
    kh = k * jnp.exp2(last - cum)
    dec = jnp.exp2(last)
    outs = []
    for j in range(nh):
        st = ch.st[j]
        vj = vs[j].astype(BF)
        rest = _join_slabs(slabs[j]) + _dot_nt(_masked(qh, masks[j]).astype(BF), st.astype(BF))
        ch.st[j] = st * dec + _dot_tn(vj, _masked(kh, masks[j]).astype(BF))
        ch.rest[j] = rest
        outs.append(rest + in_block[j])
    write_out(outs)

    def fixup():
        @pl.when(jnp.logical_not(span <= SAFE_LOG2_SPAN))
        def _():
            q_, k_, cum_ = ch.q[...], ch.k[...], ch.cum[...]
            vs_ = [ref[:, lane0:lane0 + LANES] for ref, lane0 in v_src]

            def rewrite(in_blk, lo):
                low = _levels(q_, k_, cum_, vs_, masks, fwd, ch.cum, lo, fast_block)
                write_out([ch.rest[j] + in_blk[j] + _join_slabs(low[j]) for j in range(nh)])

            def pairwise():
                rewrite(_in_block_exact(q_, k_, cum_, v_src, fwd, ch.cum, ch.k), bs)

            if mid_block is None:
                pairwise()
            else:
                first_m = 0 if fwd else mid_block - 1
                xm = cum_ - _bcast_rows(ch.cum, lambda b: b * mid_block + first_m, mid_block, cs // mid_block)
                span_m = jnp.max(-xm)
                pl.when(span_m <= SAFE_LOG2_SPAN)(
                    lambda: rewrite(_in_block_factored(q_, k_, xm, vs_, masks, fwd, mid_block), mid_block))
                pl.when(jnp.logical_not(span_m <= SAFE_LOG2_SPAN))(pairwise)

    return fixup


def _join_slabs(slabs):
    if all(sl is None for sl in slabs):
        return 0.0
    return jnp.concatenate([jnp.zeros((SCAN_BLOCK, LANES), F32) if sl is None else sl for sl in slabs],
                           axis=0)


def _hg_kernel(qf_ref, vf_ref, zf_ref, qb_ref, vb_ref, zb_ref, lbp_ref, of_ref, ob_ref, *scratch, layer):
    @pl.when(pl.program_id(2) == 0)
    def _():
        scratch[0][...] = jnp.zeros_like(scratch[0])

    blocks = (HG_MID_BLOCK, None) if layer == 0 else (HG_FAST_BLOCK, HG_MID_BLOCK)
    fixups = []
    for dirn, (q_ref, v_ref, z_ref, o_ref) in enumerate(((qf_ref, vf_ref, zf_ref, of_ref),
                                                         (qb_ref, vb_ref, zb_ref, ob_ref))):
        p = lbp_ref[dirn]
        e = jnp.exp(p - jnp.max(p, axis=0, keepdims=True))
        den = jnp.sum(e, axis=0, keepdims=True)
        lb_all = jnp.zeros_like(den)
        for i in range(1, layer + 1):
            lb_all = lb_all + e[i:i + 1]
        lb_all = lb_all / den
        for hh in range(HG_STEP_HEADS):
            lanes = slice(hh * LANES, (hh + 1) * LANES)
            lb = lb_all[:, lanes]
            z = z_ref[:, lanes]
            en = jnp.exp(-jnp.abs(z))
            big = 1.0 / (1.0 + en)
            small = en * big
            sig = jnp.where(z >= 0, big, small)
            nsig = jnp.where(z >= 0, small, big)
            k = (1.0 - lb) * nsig
            lg = jnp.log(lb + (1.0 - lb) * sig)
            qr = q_ref[:, lanes]
            q = qr * _sigmoid(qr)

            def write_out(outs, o_ref=o_ref, lanes=lanes):
                o_ref[:, lanes] = outs[0]

            ch = _Chain(scratch, dirn * HG_STEP_HEADS + hh, 1)
            fixups.append(_scan_chunk(q, k, lg, [v_ref[:, lanes]], [(v_ref, hh * LANES)], [None], dirn,
                                      ch, write_out, *blocks))
    for fixup in fixups:
        fixup()


def _gla_kernel(qf_ref, kf_ref, vf_ref, rf_ref, cosf_ref, sinf_ref,
                qb_ref, kb_ref, vb_ref, rb_ref, cosb_ref, sinb_ref, up_ref, gb_ref, of_ref, ob_ref, *scratch):
    @pl.when(pl.program_id(2) == 0)
    def _():
        scratch[0][...] = jnp.zeros_like(scratch[0])

    cs = qf_ref.shape[0]
    lane = lax.broadcasted_iota(jnp.int32, (cs, LANES), 1)
    first = (lane & 16) == 0
    lane1 = lax.broadcasted_iota(jnp.int32, (1, LANES), 1)
    masks = [jnp.where(lane1 < GLA_HEAD_K, 1.0, 0.0), jnp.where(lane1 >= GLA_HEAD_K, 1.0, 0.0)]
    fixups = []
    for dirn, (q_ref, k_ref, v_ref, r_ref, cos_ref, sin_ref, o_ref) in enumerate(
            ((qf_ref, kf_ref, vf_ref, rf_ref, cosf_ref, sinf_ref, of_ref),
             (qb_ref, kb_ref, vb_ref, rb_ref, cosb_ref, sinb_ref, ob_ref))):
        cos = cos_ref[...]
        sin = sin_ref[...]

        def rope(x):
            partner = jnp.where(first, pltpu.roll(x, LANES - 16, 1), pltpu.roll(x, 16, 1))
            return x * cos + partner * sin

        q = rope(q_ref[...] * (GLA_HEAD_K ** -0.5))
        k = rope(k_ref[...])
        pre = _dot(r_ref[...].astype(BF), up_ref[dirn].astype(BF)) + gb_ref[dirn]
        lg = (jnp.minimum(pre, 0.0) - jnp.log(1.0 + jnp.exp(-jnp.abs(pre)))) * (1.0 / GLA_GATE_NORM)

        def write_out(outs, o_ref=o_ref):
            o_ref[...] = jnp.concatenate(outs, axis=1)

        ch = _Chain(scratch, dirn, 2)
        fixups.append(_scan_chunk(q, k, lg, [v_ref[:, :LANES], v_ref[:, LANES:]],
                                  [(v_ref, 0), (v_ref, LANES)], masks, dirn, ch, write_out, GLA_FAST_BLOCK))
    for fixup in fixups:
        fixup()


HG_STEP_HEADS = 4


def _scan_scratch(n_chains, nh):
    cs = SCAN_CHUNK
    return [pltpu.VMEM((n_chains * nh, LANES, LANES), F32), pltpu.VMEM((n_chains, cs, LANES), F32),
            pltpu.VMEM((n_chains, cs, LANES), F32), pltpu.VMEM((n_chains, cs, LANES), F32),
            pltpu.VMEM((n_chains * nh, cs, LANES), F32)]


def _scan_rowblks(nc, ctx_blk0):
    fwd = lambda b, c: jnp.where(c == 0, ctx_blk0 + b, b * nc + c - 1)
    bwd = lambda b, c: jnp.where(c == 0, ctx_blk0 + b, b * nc + nc - c)
    return fwd, bwd


def _hg_scan(p, lbp, *, layer, n_batch, seq, n_all):
    cs = SCAN_CHUNK
    nc = seq // cs
    w = HG_STEP_HEADS * LANES
    rbs = _scan_rowblks(nc, (n_batch * seq) // cs)
    depth = lbp.shape[1]
    col = lambda c0: c0 // w

    def specs(rb, zcol):
        return [pl.BlockSpec((cs, w), lambda b, h, c: (rb(b, c), col(COL_HG_Q) + h)),
                pl.BlockSpec((cs, w), lambda b, h, c: (rb(b, c), col(COL_HG_I) + h)),
                pl.BlockSpec((cs, w), lambda b, h, c: (rb(b, c), col(zcol) + h))]

    return pl.pallas_call(
        functools.partial(_hg_kernel, layer=layer),
        grid=(n_batch, HG_HEADS // HG_STEP_HEADS, nc + 1),
        in_specs=specs(rbs[0], COL_HG_ZF) + specs(rbs[1], COL_HG_ZF + HG_WIDTH)
        + [pl.BlockSpec((2, depth, w), lambda b, h, c: (0, 0, h))],
        out_specs=[pl.BlockSpec((cs, w), lambda b, h, c: (rbs[0](b, c), h)),
                   pl.BlockSpec((cs, w), lambda b, h, c: (rbs[1](b, c), h))],
        out_shape=[jax.ShapeDtypeStruct((n_all, HG_WIDTH), F32)] * 2,
        scratch_shapes=_scan_scratch(2 * HG_STEP_HEADS, 1),
        compiler_params=_params(("parallel", "parallel", "arbitrary")),
        name="hg_scan",
    )(p, p, p, p, p, p, lbp)


def _gla_scan(p, r, up_pad, gate_b, cos_t, sin_t, *, n_batch, seq, n_all):
    cs = SCAN_CHUNK
    nc = seq // cs
    rbs = _scan_rowblks(nc, (n_batch * seq) // cs)
    tbs = (lambda b, c: jnp.where(c == 0, nc, c - 1), lambda b, c: jnp.where(c == 0, nc, nc - c))

    def specs(rb, tb):
        return [pl.BlockSpec((cs, LANES), lambda b, h, c: (rb(b, c), COL_GL_Q // LANES + h)),
                pl.BlockSpec((cs, LANES), lambda b, h, c: (rb(b, c), COL_GL_K // LANES + h)),
                pl.BlockSpec((cs, 2 * LANES), lambda b, h, c: (rb(b, c), COL_GL_V // (2 * LANES) + h)),
                pl.BlockSpec((cs, LANES), lambda b, h, c: (rb(b, c), 0)),
                pl.BlockSpec((cs, LANES), lambda b, h, c: (tb(b, c), 0)),
                pl.BlockSpec((cs, LANES), lambda b, h, c: (tb(b, c), 0))]

    return pl.pallas_call(
        _gla_kernel,
        grid=(n_batch, GLA_HEADS // 2, nc + 1),
        in_specs=specs(rbs[0], tbs[0]) + specs(rbs[1], tbs[1])
        + [pl.BlockSpec((2, LANES, LANES), lambda b, h, c: (0, 0, h)),
           pl.BlockSpec((2, 1, LANES), lambda b, h, c: (0, 0, h))],
        out_specs=[pl.BlockSpec((cs, 2 * LANES), lambda b, h, c: (rbs[0](b, c), h)),
                   pl.BlockSpec((cs, 2 * LANES), lambda b, h, c: (rbs[1](b, c), h))],
        out_shape=[jax.ShapeDtypeStruct((n_all, GLA_WIDTH), F32)] * 2,
        scratch_shapes=_scan_scratch(2, 2),
        compiler_params=_params(("parallel", "parallel", "arbitrary")),
        name="gla_scan",
    )(p, p, p, r, cos_t, sin_t, p, p, p, r, cos_t, sin_t, up_pad, gate_b)


def _rope_tables(seq, ctx_len):
    half = GLA_HEAD_K // 2
    inv = ROPE_BASE ** (-jnp.arange(0, half, 2, dtype=F32) / half)
    pos = jnp.arange(seq)
    ang_r = (pos // GRID_W).astype(F32)[:, None] * inv
    ang_c = (pos % GRID_W).astype(F32)[:, None] * inv
    cos = jnp.concatenate([jnp.cos(ang_r)] * 2 + [jnp.cos(ang_c)] * 2, axis=1)
    sin = jnp.concatenate([-jnp.sin(ang_r), jnp.sin(ang_r), -jnp.sin(ang_c), jnp.sin(ang_c)], axis=1)
    cos = jnp.concatenate([jnp.tile(cos, (1, 2)), jnp.ones((ctx_len, LANES), F32)], axis=0)
    sin = jnp.concatenate([jnp.tile(sin, (1, 2)), jnp.zeros((ctx_len, LANES), F32)], axis=0)
    return cos, sin


def _outproj_kernel(na_ref, nc_ref, hgf_ref, hgb_ref, ghg_ref, glf_ref, glb_ref, ggl_ref, nhg_ref, ngl_ref,
                    w_ref, x_ref, gt_ref, sh_ref, sc_ref, n2_ref, x1_ref, h2_ref, edge_ref, mix_ref, *,
                    n_lat_tiles):
    @pl.when(pl.program_id(0) < n_lat_tiles)
    def _():
        mix_ref[:, :NA_WIDTH] = na_ref[...]

    @pl.when(pl.program_id(0) >= n_lat_tiles)
    def _():
        mix_ref[:, :NA_WIDTH] = nc_ref[...]

    def norm_gate(of_ref, ob_ref, g_ref, gain_ref, col0, n_heads):
        o = of_ref[...] + ob_ref[...]
        g = g_ref[...]
        for h in range(n_heads):
            sl = slice(h * LANES, (h + 1) * LANES)
            gh = g[:, sl]
            y = _rms(o[:, sl]) * gain_ref[...]
            mix_ref[:, col0 + h * LANES:col0 + (h + 1) * LANES] = (y * (gh * _sigmoid(gh))).astype(BF)

    norm_gate(hgf_ref, hgb_ref, ghg_ref, nhg_ref, NA_WIDTH, HG_HEADS)
    norm_gate(glf_ref, glb_ref, ggl_ref, ngl_ref, NA_WIDTH + HG_WIDTH, GLA_HEADS)
    x1 = x_ref[...] + gt_ref[0] * _dot(mix_ref[...], w_ref[0])
    x1_ref[...] = x1
    y = _rms(x1) * n2_ref[...]
    h2 = y * (1.0 + sc_ref[0]) + sh_ref[0]
    h2_ref[...] = h2.astype(BF)
    edge_ref[0] = jnp.concatenate([h2[:1], h2[-1:]], axis=0)


OUT_TM = 512


def _outproj(na_o, na_ctx, o_hg, o_gl, p, hg_gain, gl_gain, w_out, x_all, mod, norm2_g, *,
             layer, n_rows, seq, n_batch):
    d = D_MODEL
    tm = OUT_TM
    tpb = seq // tm
    n_lat_tiles = n_batch * tpb
    mod_row = lambda k: (lambda i: (jnp.minimum(i // tpb, n_batch) * 6 + k, 0, 0))
    return pl.pallas_call(
        functools.partial(_outproj_kernel, n_lat_tiles=n_lat_tiles),
        grid=(n_rows // tm,),
        in_specs=[pl.BlockSpec((tm, NA_WIDTH), lambda i: (jnp.minimum(i, n_lat_tiles - 1), 0)),
                  pl.BlockSpec((tm, NA_WIDTH), lambda i: (jnp.maximum(i - n_lat_tiles, 0), 0)),
                  pl.BlockSpec((tm, HG_WIDTH), lambda i: (i, 0)),
                  pl.BlockSpec((tm, HG_WIDTH), lambda i: (i, 0)),
                  pl.BlockSpec((tm, HG_WIDTH), lambda i: (i, COL_HG_G // HG_WIDTH)),
                  pl.BlockSpec((tm, GLA_WIDTH), lambda i: (i, 0)),
                  pl.BlockSpec((tm, GLA_WIDTH), lambda i: (i, 0)),
                  pl.BlockSpec((tm, GLA_WIDTH), lambda i: (i, COL_GL_G // GLA_WIDTH)),
                  pl.BlockSpec((1, LANES), lambda i: (0, 0)),
                  pl.BlockSpec((1, LANES), lambda i: (0, 0)),
                  pl.BlockSpec((1, d, d), lambda i: (layer, 0, 0), pipeline_mode=pl.Buffered(1)),
                  pl.BlockSpec((tm, d), lambda i: (i, 0)),
                  pl.BlockSpec((1, 1, d), mod_row(2)),
                  pl.BlockSpec((1, 1, d), mod_row(3)),
                  pl.BlockSpec((1, 1, d), mod_row(4)),
                  pl.BlockSpec((1, d), lambda i: (0, 0))],
        out_specs=[pl.BlockSpec((tm, d), lambda i: (i, 0)),
                   pl.BlockSpec((tm, d), lambda i: (i, 0)),
                   pl.BlockSpec((1, 2, d), lambda i: (i, 0, 0))],
        out_shape=[jax.ShapeDtypeStruct((n_rows, d), F32),
                   jax.ShapeDtypeStruct((n_rows, d), BF),
                   jax.ShapeDtypeStruct((n_rows // tm, 2, d), F32)],
        scratch_shapes=[pltpu.VMEM((tm, d), BF)],
        compiler_params=_params(("parallel",)),
        name="outproj",
    )(na_o, na_ctx, o_hg[0], o_hg[1], p, o_gl[0], o_gl[1], p, hg_gain, gl_gain, w_out, x_all, mod, mod, mod,
      norm2_g)


def _patch_row(x, r, row_in_slab, value, keep=None):
    r0 = (r // SUBLANES) * SUBLANES
    slab = x[r0:r0 + SUBLANES]
    new = value if keep is None else slab * keep + value * (1.0 - keep)
    slab = jnp.where(row_in_slab == r - r0, new, slab)
    parts = [x[:r0]] * (r0 > 0) + [slab] + [x[r0 + SUBLANES:]] * (r0 + SUBLANES < x.shape[0])
    return jnp.concatenate(parts, axis=0)


def _ffn_kernel(h_ref, halo_ref, wa_ref, wb_ref, cwa_ref, cwb_ref, cba_ref, cbb_ref, wd_ref,
                x1_ref, gt_ref, fg_ref, o_ref, acc_ref, hc_ref, *, n_lat_tiles, ctx_len, final_norm):
    i = pl.program_id(0)
    j = pl.program_id(1)
    tm = h_ref.shape[0]
    tf = wa_ref.shape[-1]

    @pl.when(j == 0)
    def _():
        acc_ref[...] = jnp.zeros_like(acc_ref)
        hc_ref[:tm] = h_ref[...]
        hc_ref[tm:] = halo_ref[0]

    hc = hc_ref[...]
    row8 = lax.broadcasted_iota(jnp.int32, (SUBLANES, tf), 0)
    keep_lat = jnp.where(i < n_lat_tiles, 1.0, 0.0)
    zero_row = jnp.zeros((1, tf), F32)

    def conv(w_ref, cw_ref, cb_ref):
        u_all = _dot(hc, w_ref[0])
        u = u_all[:tm]
        prev = _patch_row(pltpu.roll(u, 1, 0), 0, row8, u_all[tm:tm + 1])
        nxt = _patch_row(pltpu.roll(u, tm - 1, 0), tm - 1, row8, u_all[tm + 1:tm + 2])
        for r in range(ctx_len, tm, ctx_len):
            prev = _patch_row(prev, r, row8, zero_row, keep_lat)
            nxt = _patch_row(nxt, r - 1, row8, zero_row, keep_lat)
        cw = cw_ref[...]
        return cw[0:1] * prev + cw[1:2] * u + cw[2:3] * nxt + cb_ref[...]

    a = conv(wa_ref, cwa_ref, cba_ref)
    b = conv(wb_ref, cwb_ref, cbb_ref)
    g = (a * _sigmoid(a) * b).astype(BF)
    acc_ref[...] += _dot(g, wd_ref[0])

    @pl.when(j == pl.num_programs(1) - 1)
    def _():
        x2 = x1_ref[...] + gt_ref[0] * acc_ref[...]
        if final_norm:
            x2 = _rms(x2) * fg_ref[...]
        o_ref[...] = x2


FFN_TM = 512
FFN_HALO = 2 * SUBLANES


def _ffn(h2, halo, w_up, conv_w, conv_b, w_down, x1, mod, final_g, *, layer, n_rows, n_lat_tiles,
         seq, n_batch, ctx_len, final_norm):
    d = D_MODEL
    tm, tf = FFN_TM, 512
    nf = D_FF // tf
    tpb = seq // tm
    mod_row = lambda i, j: (jnp.minimum(i // tpb, n_batch) * 6 + 5, 0, 0)
    return pl.pallas_call(
        functools.partial(_ffn_kernel, n_lat_tiles=n_lat_tiles, ctx_len=ctx_len,
                          final_norm=final_norm),
        grid=(n_rows // tm, nf),
        in_specs=[pl.BlockSpec((tm, d), lambda i, j: (i, 0)),
                  pl.BlockSpec((1, FFN_HALO, d), lambda i, j: (i, 0, 0)),
                  pl.BlockSpec((1, d, tf), lambda i, j: (layer, 0, j)),
                  pl.BlockSpec((1, d, tf), lambda i, j: (layer, 0, nf + j)),
                  pl.BlockSpec((3, tf), lambda i, j: (0, j)),
                  pl.BlockSpec((3, tf), lambda i, j: (0, nf + j)),
                  pl.BlockSpec((1, tf), lambda i, j: (0, j)),
                  pl.BlockSpec((1, tf), lambda i, j: (0, nf + j)),
                  pl.BlockSpec((1, tf, d), lambda i, j: (layer, j, 0)),
                  pl.BlockSpec((tm, d), lambda i, j: (i, 0)),
                  pl.BlockSpec((1, 1, d), mod_row),
                  pl.BlockSpec((1, d), lambda i, j: (0, 0))],
        out_specs=pl.BlockSpec((tm, d), lambda i, j: (i, 0)),
        out_shape=jax.ShapeDtypeStruct((n_rows, d), F32),
        scratch_shapes=[pltpu.VMEM((tm, d), F32), pltpu.VMEM((tm + FFN_HALO, d), BF)],
        compiler_params=_params(("parallel", "arbitrary")),
        name="ffn",
    )(h2, halo, w_up, w_up, conv_w, conv_w, conv_b, conv_b, w_down, x1, mod, final_g)


def _ffn_halo(edges, n_tiles, n_lat, seq, ctx_len):
    k = FFN_TM // OUT_TM
    d = edges.shape[-1]
    starts = np.arange(n_tiles) * FFN_TM
    seq_of = np.where(starts < n_lat, seq, ctx_len)
    rel = np.where(starts < n_lat, starts, starts - n_lat)
    prev_ok = jnp.asarray((rel % seq_of != 0).astype(np.float32))[:, None]
    next_ok = jnp.asarray(((rel + FFN_TM) % seq_of != 0).astype(np.float32))[:, None]
    zero = jnp.zeros((1, d), edges.dtype)
    last_rows = edges[k - 1::k, 1][:n_tiles]
    first_rows = edges[0::k, 0][:n_tiles]
    prev = jnp.concatenate([zero, last_rows[:-1]], axis=0) * prev_ok
    nxt = jnp.concatenate([first_rows[1:], zero], axis=0) * next_ok
    pad = jnp.zeros((n_tiles, FFN_HALO - 2, d), edges.dtype)
    return jnp.concatenate([prev[:, None], nxt[:, None], pad], axis=1).astype(BF)


def kernel(x, c, ctx, c_ctx, ada_w, ada_b, norm1_g, w_in, na_rpb, hg_lower_bounds, hg_norm_g,
           gla_gate_up, gla_gate_b, gla_norm_g, w_out, norm2_g, w_up, conv_w, conv_b, w_down, final_g):
    n_batch, seq, d = x.shape
    ctx_len = ctx.shape[1]
    depth = ada_w.shape[0]
    n_lat = n_batch * seq
    n_all = n_lat + n_batch * ctx_len
    assert d == D_MODEL and ctx_len == SCAN_CHUNK and seq % (NA_QROWS * GRID_W) == 0
    assert seq // GRID_W >= NA_KROWS + NA_QROWS and n_batch < SUBLANES

    x_all = jnp.concatenate([x.reshape(n_lat, d), ctx.reshape(n_batch * ctx_len, d)], axis=0)
    c_all = jnp.concatenate([c, c_ctx[None], jnp.zeros((SUBLANES - n_batch - 1, d), F32)], axis=0)
    mods = _ada(c_all, ada_w, ada_b)
    cos_t, sin_t = _rope_tables(seq, ctx_len)
    w_in_b, w_out_b, w_up_b, w_down_b = (w.astype(BF) for w in (w_in, w_out, w_up, w_down))

    for l in range(depth):
        last = l == depth - 1
        mod = mods[l].reshape(SUBLANES * 6, 1, d)
        w_rank = jnp.pad(w_in[l, :, MAIN_COLS:], ((0, 0), (0, LANES - 2 * GLA_RANK))).astype(BF)
        p_na, p, r = _inproj(x_all, norm1_g[l][None], mod, w_in_b, w_rank, layer=l, seq=seq, n_batch=n_batch)

        bias = _na_bias_table(na_rpb[l], seq // GRID_W)
        na_o = _na(p_na, bias, n_batch=n_batch, seq=seq, ctx_len=ctx_len)
        na_ctx = na_o if last else _cattn(p_na, n_batch=n_batch, seq=seq, ctx_len=ctx_len)

        o_hg = _hg_scan(p, hg_lower_bounds, layer=l, n_batch=n_batch, seq=seq, n_all=n_all)
        up = gla_gate_up[l]
        up_pad = jnp.zeros((2, LANES, GLA_KEY_WIDTH), F32)
        up_pad = up_pad.at[0, :GLA_RANK].set(up[0]).at[1, GLA_RANK:2 * GLA_RANK].set(up[1])
        o_gl = _gla_scan(p, r, up_pad, gla_gate_b[l][:, None, :], cos_t, sin_t,
                         n_batch=n_batch, seq=seq, n_all=n_all)

        n_rows = n_lat if last else n_all
        x1, h2, edges = _outproj(na_o, na_ctx, o_hg, o_gl, p, hg_norm_g[l][None], gla_norm_g[l][None],
                                 w_out_b, x_all, mod, norm2_g[l][None], layer=l, n_rows=n_rows,
                                 seq=seq, n_batch=n_batch)
        halo = _ffn_halo(edges, n_rows // FFN_TM, n_lat, seq, ctx_len)
        x_all = _ffn(h2, halo, w_up_b, conv_w[l], conv_b[l][None], w_down_b,
                     x1, mod, final_g[None], layer=l, n_rows=n_rows, n_lat_tiles=n_lat // FFN_TM,
                     seq=seq, n_batch=n_batch, ctx_len=ctx_len, final_norm=last)
    return x_all[:n_lat].reshape(n_batch, seq, d)
```
